```python
import math
import jax
import jax.numpy as jnp
from jax import lax
import numpy as np

D_MODEL = 1024
BATCH = 32
SEQ = 2048
DEPTH = 1

GRID_W = 64
CTX_LEN = 256
EPS = 1e-6
ROPE_BASE = 10000.0

RET_HEADS = 4
RET_DK = 128
RET_DV = 256
RET_CHUNK = 128
RET_QK_W = RET_HEADS * RET_DK
RET_V_W = RET_HEADS * RET_DV

DIFF_HEADS = 8
DIFF_D = 64
DIFF_QK_W = DIFF_HEADS * 2 * DIFF_D
DIFF_V_W = DIFF_HEADS * 2 * DIFF_D
Q_BLOCK = 128

IN_WIDTHS = (RET_QK_W, RET_QK_W, RET_V_W, RET_V_W, DIFF_QK_W, DIFF_QK_W, DIFF_V_W, D_MODEL, D_MODEL)
IN_COLS = 2 * RET_QK_W + 2 * RET_V_W + 2 * DIFF_QK_W + DIFF_V_W + 2 * D_MODEL

PEER_HEADS = 8
PEER_NKEYS = 128
PEER_EXPERTS = PEER_NKEYS * PEER_NKEYS
PEER_DKEY = 256
PEER_HALF = PEER_DKEY // 2
PEER_TOPK = 16
PEER_TOK_BLOCK = 128

kernel_name = 'hybrid_retention_diffattn_peer_dit_block'


def rmsnorm(x, g):
    xf = x.astype(jnp.float32)
    y = xf * lax.rsqrt(jnp.mean(xf * xf, axis=-1, keepdims=True) + EPS)
    return (y * g.astype(jnp.float32)).astype(x.dtype)


def modulate(h, shift, scale):
    return h * (1 + scale) + shift


def modulation(cvec, w, b):
    m = jax.nn.silu(cvec) @ w + b
    return [t[:, None, :] for t in jnp.split(m, 6, axis=-1)]


def split_cols(p):
    outs, off = [], 0
    for w in IN_WIDTHS:
        outs.append(p[..., off:off + w])
        off += w
    return outs


def rope_1d(x, pos):
    half = x.shape[-1] // 2
    inv = ROPE_BASE ** (-jnp.arange(half, dtype=jnp.float32) / half)
    ang = pos[:, None] * inv[None, :]
    cos = jnp.cos(ang).astype(x.dtype)
    sin = jnp.sin(ang).astype(x.dtype)
    x1, x2 = x[..., :half], x[..., half:]
    return jnp.concatenate([x1 * cos - x2 * sin, x1 * sin + x2 * cos], axis=-1)


def axial_rope(x, row, col):
    h = x.shape[-1] // 2
    return jnp.concatenate([rope_1d(x[..., :h], row), rope_1d(x[..., h:], col)], axis=-1)


def heads(p, h):
    B, T, _ = p.shape
    return p.reshape(B, T, h, -1).transpose(0, 2, 1, 3)


def diff_heads(p):
    B, T, _ = p.shape
    return p.reshape(B, T, DIFF_HEADS, 2, DIFF_D).transpose(0, 2, 3, 1, 4)


def flip_t(t):
    return jnp.flip(t, axis=2)


def decay_logs(log2_decay):
    return jnp.log1p(-jnp.exp2(log2_decay.astype(jnp.float32)))


def retention_scan(q, k, v, log2_decay, s0, include_diag):
    B, H, T, DK = q.shape
    DV = v.shape[-1]
    C = RET_CHUNK
    n = T // C
    log_g = decay_logs(log2_decay)
    idx = jnp.arange(C, dtype=jnp.float32)
    dist = idx[:, None] - idx[None, :]
    mask = dist >= 0 if include_diag else dist > 0
    dmat = jnp.where(mask[None], jnp.exp(log_g[:, None, None] * jnp.where(mask, dist, 0.0)[None]), 0.0)
    q_dec = jnp.exp(log_g[:, None] * (idx + 1.0))[None, :, :, None]
    k_dec = jnp.exp(log_g[:, None] * (C - 1.0 - idx))[None, :, :, None]
    s_dec = jnp.exp(log_g * C)[None, :, None, None]

    def chunks(t):
        return jnp.moveaxis(t.astype(jnp.float32).reshape(B, H, n, C, t.shape[-1]), 2, 0)

    def step(s, inp):
        qi, ki, vi = inp
        att = jnp.einsum('bhid,bhjd->bhij', qi, ki) * dmat
        y = jnp.einsum('bhij,bhjv->bhiv', att, vi) + jnp.einsum('bhid,bhdv->bhiv', qi * q_dec, s)
        s = s_dec * s + jnp.einsum('bhjd,bhjv->bhdv', ki * k_dec, vi)
        return s, y

    s_fin, ys = lax.scan(step, s0, (chunks(q), chunks(k), chunks(v)))
    return jnp.moveaxis(ys, 0, 2).reshape(B, H, T, DV), s_fin


def retention_state(k, v, log2_decay):
    T = k.shape[2]
    log_g = decay_logs(log2_decay)
    w = jnp.exp(log_g[:, None] * (T - 1.0 - jnp.arange(T, dtype=jnp.float32)))
    return jnp.einsum('bhtd,bhtv->bhdv', k.astype(jnp.float32) * w[None, :, :, None], v.astype(jnp.float32))


def bidir_retention(q, k, v, dec_f, dec_b, s_f, s_b):
    y_f, sf_fin = retention_scan(q, k, v, dec_f, s_f, True)
    y_b, sb_fin = retention_scan(flip_t(q), flip_t(k), flip_t(v), dec_b, s_b, False)
    return y_f + flip_t(y_b), sf_fin, sb_fin


def retention_out(y, g_pre, gn_g, w_br, dtype):
    B, H, T, DV = y.shape
    mu = jnp.mean(y, axis=-1, keepdims=True)
    var = jnp.var(y, axis=-1, keepdims=True)
    yn = ((y - mu) * lax.rsqrt(var + EPS)).transpose(0, 2, 1, 3).reshape(B, T, H * DV)
    yn = (yn * gn_g.astype(jnp.float32)).astype(dtype)
    return (jax.nn.silu(g_pre) * yn) @ w_br


def diff_lambda_value(lp, lam_init):
    lp = lp.astype(jnp.float32)
    return jnp.exp(jnp.sum(lp[0] * lp[1])) - jnp.exp(jnp.sum(lp[2] * lp[3])) + lam_init


def diff_attention(q, k, v, lam):
    s = jnp.einsum('bhaqd,bhakd->bhaqk', q, k).astype(jnp.float32) * (DIFF_D ** -0.5)
    p = jax.nn.softmax(s, axis=-1)
    a = p[:, :, 0] - lam * p[:, :, 1]
    return jnp.einsum('bhqk,bhkv->bhqv', a, v.astype(jnp.float32))


def diff_attention_blocks(q, k, v, lam):
    B, H, _, T, d = q.shape
    nb = T // Q_BLOCK
    qb = jnp.moveaxis(q.reshape(B, H, 2, nb, Q_BLOCK, d), 3, 0)
    o = lax.map(lambda qi: diff_attention(qi, k, v, lam), qb)
    return jnp.moveaxis(o, 0, 2).reshape(B, H, T, v.shape[-1])


def diff_out(o, gn_g, lam_init, w_br, dtype):
    B, H, T, DV = o.shape
    on = o * lax.rsqrt(jnp.mean(o * o, axis=-1, keepdims=True) + EPS) * (1.0 - lam_init)
    on = on.transpose(0, 2, 1, 3).reshape(B, T, H * DV) * gn_g.astype(jnp.float32)
    return on.astype(dtype) @ w_br


def peer_ffn(h, w_q, sub_keys, u_tab, v_tab):
    B, T, D = h.shape
    nb = (B * T) // PEER_TOK_BLOCK
    hb = h.reshape(nb, PEER_TOK_BLOCK, D)
    E = PEER_HEADS * PEER_TOPK

    def block(xb):
        q = (xb @ w_q).reshape(PEER_TOK_BLOCK, PEER_HEADS, 2, PEER_HALF)
        sc = jnp.einsum('thad,hakd->thak', q, sub_keys).astype(jnp.float32)
        s_top, i_top = lax.top_k(sc, PEER_TOPK)
        cand_s = (s_top[:, :, 0, :, None] + s_top[:, :, 1, None, :]).reshape(PEER_TOK_BLOCK, PEER_HEADS, PEER_TOPK * PEER_TOPK)
        cand_i = (i_top[:, :, 0, :, None] * PEER_NKEYS + i_top[:, :, 1, None, :]).reshape(PEER_TOK_BLOCK, PEER_HEADS, PEER_TOPK * PEER_TOPK)
        s_fin, pos = lax.top_k(cand_s, PEER_TOPK)
        idx = jnp.take_along_axis(cand_i, pos, axis=-1).reshape(PEER_TOK_BLOCK, E)
        g = jax.nn.softmax(s_fin, axis=-1).reshape(PEER_TOK_BLOCK, E)
        act = jax.nn.gelu(jnp.einsum('td,ted->te', xb, u_tab[idx]).astype(jnp.float32), approximate=False)
        return jnp.einsum('te,ted->td', (g * act).astype(xb.dtype), v_tab[idx])

    return lax.map(block, hb).reshape(B, T, D)


def setup_inputs(seed: int = 0) -> dict:
    key = jax.random.key(seed)
    ks = jax.random.split(key, 23)
    f32 = jnp.float32

    def nrm(k, shape, s):
        return jax.random.normal(k, shape, f32) * s

    def gain(k, shape):
        return 1.0 + 0.05 * jax.random.normal(k, shape, f32)

    decay_base = -5.0 - jnp.arange(RET_HEADS, dtype=f32)
    return {
        'x': nrm(ks[0], (BATCH, SEQ, D_MODEL), 1.0),
        'c': nrm(ks[1], (BATCH, D_MODEL), 1.0),
        'ctx': nrm(ks[2], (BATCH, CTX_LEN, D_MODEL), 1.0),
        'c_ctx': nrm(ks[3], (D_MODEL,), 1.0),
        'w_mod': nrm(ks[4], (DEPTH, D_MODEL, 6 * D_MODEL), 0.5 * D_MODEL ** -0.5),
        'b_mod': nrm(ks[5], (DEPTH, 6 * D_MODEL), 0.02),
        'pre_mix_g': gain(ks[6], (DEPTH, D_MODEL)),
        'post_mix_g': gain(ks[7], (DEPTH, D_MODEL)),
        'pre_ffn_g': gain(ks[8], (DEPTH, D_MODEL)),
        'post_ffn_g': gain(ks[9], (DEPTH, D_MODEL)),
        'w_in': nrm(ks[10], (DEPTH, D_MODEL, IN_COLS), D_MODEL ** -0.5),
        'ret_decay_fwd': decay_base[None] + nrm(ks[11], (DEPTH, RET_HEADS), 0.1),
        'ret_decay_bwd': decay_base[None] + nrm(ks[12], (DEPTH, RET_HEADS), 0.1),
        'ret_gn_g': gain(ks[13], (DEPTH, RET_V_W)),
        'diff_lambda': nrm(ks[14], (DEPTH, 4, DIFF_D), 0.1),
        'diff_gn_g': gain(ks[15], (DEPTH, DIFF_V_W)),
        'w_br_ret': nrm(ks[16], (DEPTH, RET_V_W, D_MODEL), RET_V_W ** -0.5),
        'w_br_diff': nrm(ks[17], (DEPTH, DIFF_V_W, D_MODEL), DIFF_V_W ** -0.5),
        'w_out': nrm(ks[18], (DEPTH, D_MODEL, D_MODEL), D_MODEL ** -0.5),
        'peer_w_q': nrm(ks[19], (DEPTH, D_MODEL, PEER_HEADS * PEER_DKEY), D_MODEL ** -0.5),
        'peer_sub_keys': nrm(ks[20], (DEPTH, PEER_HEADS, 2, PEER_NKEYS, PEER_HALF), PEER_HALF ** -0.5),
        'peer_u': nrm(ks[21], (DEPTH, PEER_EXPERTS, D_MODEL), D_MODEL ** -0.5),
        'peer_v': nrm(ks[22], (DEPTH, PEER_EXPERTS, D_MODEL), D_MODEL ** -0.5),
    }


def reference(x, c, ctx, c_ctx, w_mod, b_mod, pre_mix_g, post_mix_g, pre_ffn_g, post_ffn_g,
              w_in, ret_decay_fwd, ret_decay_bwd, ret_gn_g, diff_lambda, diff_gn_g,
              w_br_ret, w_br_diff, w_out, peer_w_q, peer_sub_keys, peer_u, peer_v):
    B, n_lat, _ = x.shape
    rows = n_lat // GRID_W
    row = jnp.repeat(jnp.arange(rows, dtype=jnp.float32), GRID_W)
    col = jnp.tile(jnp.arange(GRID_W, dtype=jnp.float32), rows)
    q_scale = RET_DK ** -0.5
    xc = ctx
    for l in range(DEPTH):
        last = l == DEPTH - 1
        lam_init = 0.8 - 0.6 * math.exp(-0.3 * l)
        lam = diff_lambda_value(diff_lambda[l], lam_init)
        sh_a, sc_a, ga_a, sh_f, sc_f, ga_f = modulation(c, w_mod[l], b_mod[l])
        csh_a, csc_a, cga_a, csh_f, csc_f, cga_f = modulation(c_ctx[None], w_mod[l], b_mod[l])

        u = modulate(rmsnorm(x, pre_mix_g[l]), sh_a, sc_a)
        uc = modulate(rmsnorm(xc, pre_mix_g[l]), csh_a, csc_a)
        rq, rk, rv, rg, dq, dk, dv, gr, gd = split_cols(u @ w_in[l])
        rq_c, rk_c, rv_c, rg_c, dq_c, dk_c, dv_c, gr_c, gd_c = split_cols(uc @ w_in[l])

        qr = axial_rope(heads(rq, RET_HEADS), row, col) * q_scale
        kr = axial_rope(heads(rk, RET_HEADS), row, col)
        vr = heads(rv, RET_HEADS)
        kr_c = heads(rk_c, RET_HEADS)
        vr_c = heads(rv_c, RET_HEADS)
        if last:
            s_f = retention_state(kr_c, vr_c, ret_decay_fwd[l])
            s_b = retention_state(flip_t(kr_c), flip_t(vr_c), ret_decay_bwd[l])
        else:
            zero = jnp.zeros((B, RET_HEADS, RET_DK, RET_DV), jnp.float32)
            qr_c = heads(rq_c, RET_HEADS) * q_scale
            yr_c, s_f, s_b = bidir_retention(qr_c, kr_c, vr_c, ret_decay_fwd[l], ret_decay_bwd[l], zero, zero)
        yr, _, _ = bidir_retention(qr, kr, vr, ret_decay_fwd[l], ret_decay_bwd[l], s_f, s_b)
        p_ret = retention_out(yr, rg, ret_gn_g[l], w_br_ret[l], x.dtype)

        qd = axial_rope(diff_heads(dq), row, col)
        kd = axial_rope(diff_heads(dk), row, col)
        vd = heads(dv, DIFF_HEADS)
        kd_c = diff_heads(dk_c)
        vd_c = heads(dv_c, DIFF_HEADS)
        k_all = jnp.concatenate([kd, kd_c], axis=3)
        v_all = jnp.concatenate([vd, vd_c], axis=2)
        p_diff = diff_out(diff_attention_blocks(qd, k_all, v_all, lam), diff_gn_g[l], lam_init, w_br_diff[l], x.dtype)

        mix = (jax.nn.sigmoid(gr) * p_ret + jax.nn.sigmoid(gd) * p_diff) @ w_out[l]
        x = x + ga_a * rmsnorm(mix, post_mix_g[l])
        if not last:
            p_ret_c = retention_out(yr_c, rg_c, ret_gn_g[l], w_br_ret[l], xc.dtype)
            p_diff_c = diff_out(diff_attention(diff_heads(dq_c), kd_c, vd_c, lam), diff_gn_g[l], lam_init, w_br_diff[l], xc.dtype)
            mix_c = (jax.nn.sigmoid(gr_c) * p_ret_c + jax.nn.sigmoid(gd_c) * p_diff_c) @ w_out[l]
            xc = xc + cga_a * rmsnorm(mix_c, post_mix_g[l])

        f = modulate(rmsnorm(x, pre_ffn_g[l]), sh_f, sc_f)
        x = x + ga_f * rmsnorm(peer_ffn(f, peer_w_q[l], peer_sub_keys[l], peer_u[l], peer_v[l]), post_ffn_g[l])
        if not last:
            fc = modulate(rmsnorm(xc, pre_ffn_g[l]), csh_f, csc_f)
            xc = xc + cga_f * rmsnorm(peer_ffn(fc, peer_w_q[l], peer_sub_keys[l], peer_u[l], peer_v[l]), post_ffn_g[l])
    return x
```

```python
import functools
import math

import jax
import jax.numpy as jnp
from jax import lax
from jax.experimental import pallas as pl
from jax.experimental.pallas import tpu as pltpu

F32 = jnp.float32
BF16 = jnp.bfloat16

D_MODEL = 1024
GRID_W = 64
EPS = 1e-6
ROPE_BASE = 10000.0

RET_HEADS = 4
RET_DK = 128
RET_DV = 256
RET_CHUNK = 128
RET_QK_W = RET_HEADS * RET_DK
RET_V_W = RET_HEADS * RET_DV

DIFF_HEADS = 8
DIFF_D = 64
DIFF_W = DIFF_HEADS * 2 * DIFF_D

PEER_HEADS = 8
PEER_NKEYS = 128
PEER_EXPERTS = PEER_NKEYS * PEER_NKEYS
PEER_HALF = 128
PEER_TOPK = 16

COL_RQ = 0
COL_RK = COL_RQ + RET_QK_W
COL_RV = COL_RK + RET_QK_W
COL_RG = COL_RV + RET_V_W
COL_DQ = COL_RG + RET_V_W
COL_DK = COL_DQ + DIFF_W
COL_DV = COL_DK + DIFF_W
COL_GR = COL_DV + DIFF_W
COL_GD = COL_GR + D_MODEL
IN_COLS = COL_GD + D_MODEL

LANES = 128
VMEM_LIMIT = 56 << 20

NEG_INF = float("-inf")
RANK_NONE = 127.0

NT_DIMS = (((1,), (1,)), ((), ()))
TN_DIMS = (((0,), (0,)), ((), ()))


def _cparams(*sem):
    return pltpu.CompilerParams(dimension_semantics=sem, vmem_limit_bytes=VMEM_LIMIT)


def _resident(shape):
    nd = len(shape)
    return pl.BlockSpec(shape, lambda *_: (0,) * nd, pipeline_mode=pl.Buffered(1))


def _mod_kernel(c_ref, w_ref, b_ref, o_ref):
    c = c_ref[...]
    s = c * jax.nn.sigmoid(c)
    o_ref[...] = jnp.dot(s, w_ref[...], preferred_element_type=F32,
                         precision=lax.Precision.HIGHEST) + b_ref[...]


def _modulation(cc, w, b):
    rows, d = cc.shape
    n = w.shape[1]
    tn = 768
    return pl.pallas_call(
        _mod_kernel,
        grid=(n // tn,),
        in_specs=[pl.BlockSpec((rows, d), lambda j: (0, 0)),
                  pl.BlockSpec((d, tn), lambda j: (0, j)),
                  pl.BlockSpec((1, tn), lambda j: (0, j))],
        out_specs=pl.BlockSpec((rows, tn), lambda j: (0, j)),
        out_shape=jax.ShapeDtypeStruct((rows, n), F32),
        compiler_params=_cparams("arbitrary"),
        name="modulation",
    )(cc, w, b.reshape(1, n))


def _rope_tables(seq, head_dim):
    rows = seq // GRID_W
    row = jnp.repeat(jnp.arange(rows, dtype=F32), GRID_W)
    col = jnp.tile(jnp.arange(GRID_W, dtype=F32), rows)
    half = head_dim // 2
    pair = half // 2
    lane = jnp.arange(LANES)
    d = lane % head_dim
    inv = ROPE_BASE ** (-jnp.arange(pair, dtype=F32) / pair)
    freq = inv[d % pair]
    pos = jnp.where((d < half)[None, :], row[:, None], col[:, None])
    ang = pos * freq[None, :]
    cos = jnp.cos(ang)
    sin = jnp.sin(ang)
    first = ((d % half) < pair)[None, :]
    sin_a = jnp.where(first, -sin, 0.0)
    sin_b = jnp.where(first, 0.0, sin)
    return cos, sin_a, sin_b, pair


def _rope(acc, cos, sin_a, sin_b, pair):
    up = pltpu.roll(acc, LANES - pair, 1)
    dn = pltpu.roll(acc, pair, 1)
    return acc * cos + up * sin_a + dn * sin_b


def _inproj_kernel(*refs, specs, rope, ret_pair, diff_pair):
    if rope:
        (x_ref, g_ref, sc_ref, sh_ref, w_ref,
         rc_ref, ra_ref, rb_ref, dc_ref, da_ref, db_ref) = refs[:11]
        out_refs = refs[11:]
    else:
        x_ref, g_ref, sc_ref, sh_ref, w_ref = refs[:5]
        out_refs = refs[5:]
    x = x_ref[...]
    y = x * lax.rsqrt(jnp.mean(x * x, axis=-1, keepdims=True) + EPS)
    u = (y * g_ref[...]) * (1.0 + sc_ref[0]) + sh_ref[0]
    ub = u.astype(BF16)
    for (col0, width, kind), o_ref in zip(specs, out_refs):
        for c in range(0, width, 512):
            cw = min(512, width - c)
            acc = jnp.dot(ub, w_ref[:, col0 + c:col0 + c + cw], preferred_element_type=F32)
            if kind in ("ret_q", "ret_k", "diff_q", "diff_k"):
                for l in range(0, cw, LANES):
                    a = acc[:, l:l + LANES]
                    if kind.startswith("ret"):
                        r = _rope(a, rc_ref[...], ra_ref[...], rb_ref[...], ret_pair)
                    else:
                        r = _rope(a, dc_ref[...], da_ref[...], db_ref[...], diff_pair)
                    if kind == "ret_q":
                        r = r * (RET_DK ** -0.5)
                    elif kind == "diff_q":
                        r = r * (DIFF_D ** -0.5)
                    o_ref[:, c + l:c + l + LANES] = r.astype(BF16)
            elif kind == "silu":
                o_ref[:, c:c + cw] = (acc * jax.nn.sigmoid(acc)).astype(BF16)
            elif kind == "sigmoid":
                o_ref[:, c:c + cw] = jax.nn.sigmoid(acc).astype(BF16)
            else:
                o_ref[:, c:c + cw] = acc.astype(BF16)


def _inproj(x2, gain, scale, shift, w_bf, specs, seq, tables):
    n, d = x2.shape
    tm = min(512, seq)
    assert seq % tm == 0 and n % seq == 0
    per_seq = seq // tm
    nb = scale.shape[0]
    if nb == 1:
        mod_map = lambda i: (0, 0, 0)
    else:
        mod_map = lambda i: (i // per_seq, 0, 0)
    rope = tables is not None
    in_specs = [pl.BlockSpec((tm, d), lambda i: (i, 0)),
                pl.BlockSpec((1, d), lambda i: (0, 0)),
                pl.BlockSpec((1, 1, d), mod_map),
                pl.BlockSpec((1, 1, d), mod_map),
                _resident(w_bf.shape)]
    args = [x2, gain.reshape(1, d), scale, shift, w_bf]
    ret_pair = diff_pair = 0
    if rope:
        (rc, ra, rb, ret_pair), (dc, da, db, diff_pair) = tables
        tab_spec = pl.BlockSpec((tm, LANES), lambda i: (i % per_seq, 0))
        in_specs += [tab_spec] * 6
        args += [rc, ra, rb, dc, da, db]
    out_specs = [pl.BlockSpec((tm, w), lambda i: (i, 0)) for (_, w, _) in specs]
    out_shape = [jax.ShapeDtypeStruct((n, w), BF16) for (_, w, _) in specs]
    kern = functools.partial(_inproj_kernel, specs=tuple(specs), rope=rope,
                             ret_pair=ret_pair, diff_pair=diff_pair)
    return pl.pallas_call(
        kern, grid=(n // tm,), in_specs=in_specs, out_specs=out_specs, out_shape=out_shape,
        compiler_params=_cparams("parallel"),
        name="inproj_rope" if rope else "inproj_ctx",
    )(*args)


def _ret_kernel(sdec_ref, q_ref, k_ref, v_ref, g_ref, kc_ref, vc_ref, m_ref, dec_ref, cdec_ref,
                gn_ref, o_ref, ybuf, sf, sb, *, n_chunks):
    C = RET_CHUNK
    h = pl.program_id(1)
    sdec_f = sdec_ref[2 * h]
    sdec_b = sdec_ref[2 * h + 1]

    kc = kc_ref[...].astype(F32)
    vc = vc_ref[...]
    sf[...] = lax.dot_general((kc * cdec_ref[0, 0]).astype(BF16), vc, TN_DIMS,
                              preferred_element_type=F32)
    sb[...] = lax.dot_general((kc * cdec_ref[0, 1]).astype(BF16), vc, TN_DIMS,
                              preferred_element_type=F32)

    dmat = m_ref[0]
    qdec_f = dec_ref[0, 0]
    kdec_f = dec_ref[0, 1]
    qdec_b = dec_ref[0, 2]
    kdec_b = dec_ref[0, 3]
    gn = gn_ref[...]

    def fwd(i, carry):
        r = pl.multiple_of(i * C, C)
        q = q_ref[pl.ds(r, C), :]
        k = k_ref[pl.ds(r, C), :]
        v = v_ref[pl.ds(r, C), :]
        qf = q.astype(F32)
        kf = k.astype(F32)
        att = lax.dot_general(q, k, NT_DIMS, preferred_element_type=F32) * dmat
        y = jnp.dot(att.astype(BF16), v, preferred_element_type=F32)
        y = y + jnp.dot((qf * qdec_f).astype(BF16), sf[...].astype(BF16),
                        preferred_element_type=F32)
        ybuf[pl.ds(r, C), :] = y
        sf[...] = sdec_f * sf[...] + lax.dot_general((kf * kdec_f).astype(BF16), v, TN_DIMS,
                                                     preferred_element_type=F32)
        return carry

    lax.fori_loop(0, n_chunks, fwd, 0)

    def bwd(j, carry):
        i = n_chunks - 1 - j
        r = pl.multiple_of(i * C, C)
        q = q_ref[pl.ds(r, C), :]
        k = k_ref[pl.ds(r, C), :]
        v = v_ref[pl.ds(r, C), :]
        qf = q.astype(F32)
        kf = k.astype(F32)
        y = ybuf[pl.ds(r, C), :] + jnp.dot((qf * qdec_b).astype(BF16), sb[...].astype(BF16),
                                           preferred_element_type=F32)
        sb[...] = sdec_b * sb[...] + lax.dot_general((kf * kdec_b).astype(BF16), v, TN_DIMS,
                                                     preferred_element_type=F32)
        mu = jnp.mean(y, axis=-1, keepdims=True)
        yc = y - mu
        var = jnp.mean(yc * yc, axis=-1, keepdims=True)
        yn = yc * lax.rsqrt(var + EPS) * gn
        o_ref[pl.ds(r, C), :] = (g_ref[pl.ds(r, C), :].astype(F32) * yn).astype(BF16)
        return carry

    lax.fori_loop(0, n_chunks, bwd, 0)


def _retention(ret_q, ret_k, ret_v, ret_g, ctx_rk, ctx_rv, dec_f, dec_b, gn_g, batch, seq, ctx_len):
    C = RET_CHUNK
    H = RET_HEADS
    lg_f = jnp.log1p(-jnp.exp2(dec_f.astype(F32)))
    lg_b = jnp.log1p(-jnp.exp2(dec_b.astype(F32)))
    idx = jnp.arange(C, dtype=F32)
    dist = idx[:, None] - idx[None, :]
    dmat = jnp.where(dist[None] >= 0,
                     jnp.exp(lg_f[:, None, None] * jnp.maximum(dist, 0.0)[None]),
                     jnp.exp(lg_b[:, None, None] * jnp.maximum(-dist, 0.0)[None]))
    qdec_f = jnp.exp(lg_f[:, None] * (idx + 1.0))
    kdec_f = jnp.exp(lg_f[:, None] * (C - 1.0 - idx))
    qdec_b = jnp.exp(lg_b[:, None] * (C - idx))
    kdec_b = jnp.exp(lg_b[:, None] * idx)
    dec = jnp.stack([qdec_f, kdec_f, qdec_b, kdec_b], axis=1)
    dec = jnp.broadcast_to(dec[..., None], (H, 4, C, RET_DK))
    cidx = jnp.arange(ctx_len, dtype=F32)
    cdec = jnp.stack([jnp.exp(lg_f[:, None] * (ctx_len - 1.0 - cidx)),
                      jnp.exp(lg_b[:, None] * cidx)], axis=1)
    cdec = jnp.broadcast_to(cdec[..., None], (H, 2, ctx_len, RET_DK))
    sdec = jnp.stack([jnp.exp(lg_f * C), jnp.exp(lg_b * C)], axis=1).reshape(2 * H)

    n_chunks = seq // C
    kern = functools.partial(_ret_kernel, n_chunks=n_chunks)
    return pl.pallas_call(
        kern,
        grid=(batch, H),
        in_specs=[pl.BlockSpec(memory_space=pltpu.SMEM),
                  pl.BlockSpec((seq, RET_DK), lambda b, h: (b, h)),
                  pl.BlockSpec((seq, RET_DK), lambda b, h: (b, h)),
                  pl.BlockSpec((seq, RET_DV), lambda b, h: (b, h)),
                  pl.BlockSpec((seq, RET_DV), lambda b, h: (b, h)),
                  pl.BlockSpec((ctx_len, RET_DK), lambda b, h: (b, h)),
                  pl.BlockSpec((ctx_len, RET_DV), lambda b, h: (b, h)),
                  pl.BlockSpec((1, C, C), lambda b, h: (h, 0, 0)),
                  pl.BlockSpec((1, 4, C, RET_DK), lambda b, h: (h, 0, 0, 0)),
                  pl.BlockSpec((1, 2, ctx_len, RET_DK), lambda b, h: (h, 0, 0, 0)),
                  pl.BlockSpec((1, RET_DV), lambda b, h: (0, h))],
        out_specs=pl.BlockSpec((seq, RET_DV), lambda b, h: (b, h)),
        out_shape=jax.ShapeDtypeStruct((batch * seq, RET_V_W), BF16),
        scratch_shapes=[pltpu.VMEM((seq, RET_DV), F32),
                        pltpu.VMEM((RET_DK, RET_DV), F32),
                        pltpu.VMEM((RET_DK, RET_DV), F32)],
        compiler_params=_cparams("parallel", "arbitrary"),
        name="retention",
    )(sdec, ret_q, ret_k, ret_v, ret_g, ctx_rk, ctx_rv, dmat, dec, cdec,
      gn_g.reshape(1, RET_V_W))


def _diff_kernel(q_ref, kl_ref, kc_ref, vl_ref, vc_ref, lp_ref, gn_ref, o_ref, *, lam_init):
    q = q_ref[...]
    lane = lax.broadcasted_iota(jnp.int32, q.shape, 1)
    zero = jnp.zeros_like(q)
    kl = kl_ref[...]
    kc = kc_ref[...]
    vl = vl_ref[...]
    vc = vc_ref[...]

    def branch(qm):
        sl = lax.dot_general(qm, kl, NT_DIMS, preferred_element_type=F32)
        sc = lax.dot_general(qm, kc, NT_DIMS, preferred_element_type=F32)
        m = jnp.maximum(jnp.max(sl, axis=-1, keepdims=True), jnp.max(sc, axis=-1, keepdims=True))
        pl_ = jnp.exp(sl - m)
        pc = jnp.exp(sc - m)
        den = jnp.sum(pl_, axis=-1, keepdims=True) + jnp.sum(pc, axis=-1, keepdims=True)
        o = (jnp.dot(pl_.astype(BF16), vl, preferred_element_type=F32)
             + jnp.dot(pc.astype(BF16), vc, preferred_element_type=F32))
        return o / den

    o1 = branch(jnp.where(lane < DIFF_D, q, zero))
    o2 = branch(jnp.where(lane >= DIFF_D, q, zero))
    lp = lp_ref[...]
    lam = (jnp.exp(jnp.sum(lp[0:1] * lp[1:2], axis=-1, keepdims=True))
           - jnp.exp(jnp.sum(lp[2:3] * lp[3:4], axis=-1, keepdims=True)) + lam_init)
    o = o1 - lam * o2
    on = o * lax.rsqrt(jnp.mean(o * o, axis=-1, keepdims=True) + EPS) * (1.0 - lam_init)
    o_ref[...] = (on * gn_ref[...]).astype(BF16)


def _diff_attention(dq, dk, dv, ctx_dk, ctx_dv, lam_params, gn_g, lam_init, batch, seq, ctx_len):
    tq = min(512, seq)
    nq = seq // tq
    hw = 2 * DIFF_D
    kern = functools.partial(_diff_kernel, lam_init=lam_init)
    return pl.pallas_call(
        kern,
        grid=(batch, DIFF_HEADS, nq),
        in_specs=[pl.BlockSpec((tq, hw), lambda b, h, i: (b * nq + i, h)),
                  pl.BlockSpec((seq, hw), lambda b, h, i: (b, h)),
                  pl.BlockSpec((ctx_len, hw), lambda b, h, i: (b, h)),
                  pl.BlockSpec((seq, hw), lambda b, h, i: (b, h)),
                  pl.BlockSpec((ctx_len, hw), lambda b, h, i: (b, h)),
                  pl.BlockSpec((4, DIFF_D), lambda b, h, i: (0, 0)),
                  pl.BlockSpec((1, hw), lambda b, h, i: (0, h))],
        out_specs=pl.BlockSpec((tq, hw), lambda b, h, i: (b * nq + i, h)),
        out_shape=jax.ShapeDtypeStruct((batch * seq, DIFF_W), BF16),
        compiler_params=_cparams("parallel", "parallel", "arbitrary"),
        name="diff_attention",
    )(dq, dk, ctx_dk, dv, ctx_dv, lam_params, gn_g.reshape(1, DIFF_W))


def _merge_kernel(x_ref, zr_ref, zd_ref, gr_ref, gd_ref, wr_ref, wd_ref, wo_ref,
                  pmg_ref, pfg_ref, ga_ref, shf_ref, scf_ref, x1_ref, f_ref):
    p_ret = jnp.dot(zr_ref[...], wr_ref[...], preferred_element_type=F32)
    p_diff = jnp.dot(zd_ref[...], wd_ref[...], preferred_element_type=F32)
    m = gr_ref[...].astype(F32) * p_ret + gd_ref[...].astype(F32) * p_diff
    mix = jnp.dot(m.astype(BF16), wo_ref[...], preferred_element_type=F32)
    mixn = mix * lax.rsqrt(jnp.mean(mix * mix, axis=-1, keepdims=True) + EPS) * pmg_ref[...]
    x1 = x_ref[...] + ga_ref[0] * mixn
    x1_ref[...] = x1
    fn = x1 * lax.rsqrt(jnp.mean(x1 * x1, axis=-1, keepdims=True) + EPS) * pfg_ref[...]
    f_ref[...] = fn * (1.0 + scf_ref[0]) + shf_ref[0]


def _merge(x2, z_ret, z_diff, g_r, g_d, w_br_ret, w_br_diff, w_out, post_mix_g, pre_ffn_g,
           ga_a, sh_f, sc_f, seq):
    n, d = x2.shape
    tm = min(512, seq)
    per_seq = seq // tm
    row = lambda i: (i, 0)
    mod = lambda i: (i // per_seq, 0, 0)
    vec = pl.BlockSpec((1, d), lambda i: (0, 0))
    return pl.pallas_call(
        _merge_kernel,
        grid=(n // tm,),
        in_specs=[pl.BlockSpec((tm, d), row)] * 5
                 + [_resident((d, d))] * 3
                 + [vec, vec]
                 + [pl.BlockSpec((1, 1, d), mod)] * 3,
        out_specs=[pl.BlockSpec((tm, d), row), pl.BlockSpec((tm, d), row)],
        out_shape=[jax.ShapeDtypeStruct((n, d), F32), jax.ShapeDtypeStruct((n, d), F32)],
        compiler_params=_cparams("parallel"),
        name="merge",
    )(x2, z_ret, z_diff, g_r, g_d, w_br_ret.astype(BF16), w_br_diff.astype(BF16),
      w_out.astype(BF16), post_mix_g.reshape(1, d), pre_ffn_g.reshape(1, d), ga_a, sh_f, sc_f)


def _staircase():
    return [(p, q) for p in range(PEER_TOPK) for q in range(PEER_TOPK)
            if (p + 1) * (q + 1) <= PEER_TOPK]


def _split_bf16(a):
    hi = a.astype(BF16)
    lo = (a - hi.astype(F32)).astype(BF16)
    return hi, lo


def _dot3(a_hi, a_lo, b_hi, b_lo, dims):
    d = lambda a, b: lax.dot_general(a, b, dims, preferred_element_type=F32)
    return d(a_hi, b_hi) + (d(a_hi, b_lo) + d(a_lo, b_hi))


def _route_kernel(f_ref, wqh_ref, wql_ref, skh_ref, skl_ref, fb_ref, r2_ref, n1_ref, e1_ref, e2_ref,
                  s_scr, work, rank, top):
    K = PEER_NKEYS
    G = 2 * PEER_HEADS
    f = f_ref[...]
    fb_ref[...] = f.astype(BF16)
    f_hi, f_lo = _split_bf16(f)
    qt = _dot3(wqh_ref[...], wql_ref[...], f_hi, f_lo, NT_DIMS)
    for g in range(G):
        q_hi, q_lo = _split_bf16(qt[g * K:(g + 1) * K, :])
        s = _dot3(skh_ref[g], skl_ref[g], q_hi, q_lo, (((1,), (0,)), ((), ())))
        s_scr[g * K:(g + 1) * K, :] = s
        work[g * K:(g + 1) * K, :] = s
    rank[...] = jnp.full(rank.shape, RANK_NONE, F32)

    def extract(p, carry):
        pf = p.astype(F32)
        for g in range(G):
            hh, a = divmod(g, 2)
            w = work[g * K:(g + 1) * K, :]
            m = jnp.max(w, axis=0, keepdims=True)
            eq = w == m
            work[g * K:(g + 1) * K, :] = jnp.where(eq, NEG_INF, w)
            if a == 1:
                rk = rank[hh * K:(hh + 1) * K, :]
                rank[hh * K:(hh + 1) * K, :] = jnp.where(eq, pf, rk)
            top[pl.ds((a * PEER_TOPK + p) * PEER_HEADS + hh, 1), :] = m
        return carry

    lax.fori_loop(0, PEER_TOPK, extract, 0)

    H = PEER_HEADS
    tops_a = [top[p * H:(p + 1) * H, :] for p in range(PEER_TOPK)]
    tops_b = [top[(PEER_TOPK + q) * H:(PEER_TOPK + q + 1) * H, :] for q in range(PEER_TOPK)]
    pairs = _staircase()
    cand = [tops_a[p] + tops_b[q] for (p, q) in pairs]
    cur = list(cand)
    tau = None
    for it in range(PEER_TOPK):
        tau = functools.reduce(jnp.maximum, cur)
        if it + 1 < PEER_TOPK:
            cur = [jnp.where(c == tau, NEG_INF, c) for c in cur]
    c00 = cand[0]
    z = functools.reduce(
        lambda a, b: a + b,
        [jnp.where(c >= tau, jnp.exp(c - c00), 0.0) for c in cand])
    zinv = 1.0 / z

    for hh in range(H):
        s1 = s_scr[(2 * hh) * K:(2 * hh + 1) * K, :]
        s2 = s_scr[(2 * hh + 1) * K:(2 * hh + 2) * K, :]
        tau_h = tau[hh:hh + 1, :]
        cnt = jnp.zeros_like(s1)
        for q in range(PEER_TOPK):
            thr = tau_h - tops_b[q][hh:hh + 1, :]
            cnt = cnt + jnp.where(s1 >= thr, 1.0, 0.0)
        n1_ref[hh] = cnt
        e1_ref[hh] = jnp.exp(s1 - tops_a[0][hh:hh + 1, :]) * zinv[hh:hh + 1, :]
        e2_ref[hh] = jnp.exp(s2 - tops_b[0][hh:hh + 1, :]).astype(BF16)
        r2_ref[hh] = rank[hh * K:(hh + 1) * K, :].astype(BF16)


def _route(f, w_q, sub_keys, seq):
    n, d = f.shape
    tt = min(256, seq)
    H, K = PEER_HEADS, PEER_NKEYS
    wq_hi, wq_lo = _split_bf16(w_q.T)
    sk_hi, sk_lo = _split_bf16(sub_keys.reshape(2 * H, K, PEER_HALF))
    tab = lambda dt: jax.ShapeDtypeStruct((H, K, n), dt)
    tab_spec = pl.BlockSpec((H, K, tt), lambda t: (0, 0, t))
    return pl.pallas_call(
        _route_kernel,
        grid=(n // tt,),
        in_specs=[pl.BlockSpec((tt, d), lambda t: (t, 0)),
                  _resident(wq_hi.shape), _resident(wq_lo.shape),
                  _resident(sk_hi.shape), _resident(sk_lo.shape)],
        out_specs=[pl.BlockSpec((tt, d), lambda t: (t, 0)), tab_spec, tab_spec, tab_spec, tab_spec],
        out_shape=[jax.ShapeDtypeStruct((n, d), BF16), tab(BF16), tab(F32), tab(F32), tab(BF16)],
        scratch_shapes=[pltpu.VMEM((2 * H * K, tt), F32),
                        pltpu.VMEM((2 * H * K, tt), F32),
                        pltpu.VMEM((H * K, tt), F32),
                        pltpu.VMEM((2 * PEER_TOPK * H, tt), F32)],
        compiler_params=_cparams("parallel"),
        name="peer_route",
    )(f, wq_hi, wq_lo, sk_hi, sk_lo)


def _gelu(x):
    return 0.5 * x * (1.0 + lax.erf(x * (2.0 ** -0.5)))


def _expert_kernel(fb_ref, u_ref, vt_ref, r2_ref, e2_ref, n1_ref, e1_ref, x1_ref, g_ref, ga_ref,
                   o_ref, acc, w_scr, *, rows_per_tile):
    e = pl.program_id(1)
    K = PEER_NKEYS

    @pl.when(e == 0)
    def _():
        acc[...] = jnp.zeros_like(acc)

    fb = fb_ref[...]
    for i in range(rows_per_tile):
        h_t = lax.dot_general(u_ref[i * K:(i + 1) * K, :], fb, NT_DIMS,
                              preferred_element_type=F32)
        act = _gelu(h_t).astype(BF16)
        gate = jnp.zeros(act.shape, BF16)
        for hh in range(PEER_HEADS):
            n1 = n1_ref[hh, i:i + 1, :].astype(BF16)
            e1 = e1_ref[hh, i:i + 1, :].astype(BF16)
            sel = r2_ref[hh] < n1
            gate = gate + jnp.where(sel, e1 * e2_ref[hh], jnp.zeros_like(gate))
        w_scr[i * K:(i + 1) * K, :] = gate * act
    acc[...] += jnp.dot(vt_ref[...], w_scr[...], preferred_element_type=F32)

    @pl.when(e == pl.num_programs(1) - 1)
    def _():
        y = acc[...]
        yn = y * lax.rsqrt(jnp.mean(y * y, axis=0, keepdims=True) + EPS)
        o_ref[...] = x1_ref[...] + ga_ref[0] * (yn.T * g_ref[...])


def _experts(fb, u_bf, vt_bf, r2, n1, e1, e2, x1, post_ffn_g, ga_f, seq):
    n, d = fb.shape
    tt = min(512, seq)
    per_seq = seq // tt
    rows_per_tile = 8
    et = rows_per_tile * PEER_NKEYS
    H, K = PEER_HEADS, PEER_NKEYS
    kern = functools.partial(_expert_kernel, rows_per_tile=rows_per_tile)
    return pl.pallas_call(
        kern,
        grid=(n // tt, PEER_EXPERTS // et),
        in_specs=[pl.BlockSpec((tt, d), lambda t, e: (t, 0)),
                  pl.BlockSpec((et, d), lambda t, e: (e, 0)),
                  pl.BlockSpec((d, et), lambda t, e: (0, e)),
                  pl.BlockSpec((H, K, tt), lambda t, e: (0, 0, t)),
                  pl.BlockSpec((H, K, tt), lambda t, e: (0, 0, t)),
                  pl.BlockSpec((H, rows_per_tile, tt), lambda t, e: (0, e, t)),
                  pl.BlockSpec((H, rows_per_tile, tt), lambda t, e: (0, e, t)),
                  pl.BlockSpec((tt, d), lambda t, e: (t, 0)),
                  pl.BlockSpec((1, d), lambda t, e: (0, 0)),
                  pl.BlockSpec((1, 1, d), lambda t, e: (t // per_seq, 0, 0))],
        out_specs=pl.BlockSpec((tt, d), lambda t, e: (t, 0)),
        out_shape=jax.ShapeDtypeStruct((n, d), F32),
        scratch_shapes=[pltpu.VMEM((d, tt), F32), pltpu.VMEM((et, tt), BF16)],
        compiler_params=_cparams("parallel", "arbitrary"),
        name="peer_experts",
    )(fb, u_bf, vt_bf, r2, e2, n1, e1, x1, post_ffn_g.reshape(1, d), ga_f)


def kernel(x, c, ctx, c_ctx, w_mod, b_mod, pre_mix_g, post_mix_g, pre_ffn_g, post_ffn_g, w_in,
           ret_decay_fwd, ret_decay_bwd, ret_gn_g, diff_lambda, diff_gn_g, w_br_ret, w_br_diff,
           w_out, peer_w_q, peer_sub_keys, peer_u, peer_v):
    batch, seq, d = x.shape
    ctx_len = ctx.shape[1]
    depth = w_mod.shape[0]
    assert depth == 1 and d == D_MODEL
    l = 0
    lam_init = 0.8 - 0.6 * math.exp(-0.3 * l)

    rows = ((batch + 1 + 7) // 8) * 8
    cc = jnp.zeros((rows, d), F32).at[:batch].set(c).at[batch].set(c_ctx)
    mod = _modulation(cc, w_mod[l], b_mod[l])
    sh_a, sc_a, ga_a, sh_f, sc_f, ga_f = [t[:batch, None, :] for t in jnp.split(mod, 6, axis=-1)]
    csh_a, csc_a = [t[batch:batch + 1, None, :] for t in jnp.split(mod, 6, axis=-1)[:2]]

    w_in_bf = w_in[l].astype(BF16)
    x2 = x.reshape(batch * seq, d)
    ctx2 = ctx.reshape(batch * ctx_len, d)

    lat_specs = [(COL_RQ, RET_QK_W, "ret_q"), (COL_RK, RET_QK_W, "ret_k"),
                 (COL_RV, RET_V_W, "plain"), (COL_RG, RET_V_W, "silu"),
                 (COL_DQ, DIFF_W, "diff_q"), (COL_DK, DIFF_W, "diff_k"),
                 (COL_DV, DIFF_W, "plain"), (COL_GR, D_MODEL, "sigmoid"),
                 (COL_GD, D_MODEL, "sigmoid")]
    tables = (_rope_tables(seq, RET_DK), _rope_tables(seq, DIFF_D))
    rq, rk, rv, rg, dq, dk, dv, g_r, g_d = _inproj(
        x2, pre_mix_g[l], sc_a, sh_a, w_in_bf, lat_specs, seq, tables)

    ctx_specs = [(COL_RK, RET_QK_W, "plain"), (COL_RV, RET_V_W, "plain"),
                 (COL_DK, DIFF_W, "plain"), (COL_DV, DIFF_W, "plain")]
    rk_c, rv_c, dk_c, dv_c = _inproj(ctx2, pre_mix_g[l], csc_a, csh_a, w_in_bf, ctx_specs,
                                     ctx_len, None)

    z_ret = _retention(rq, rk, rv, rg, rk_c, rv_c, ret_decay_fwd[l], ret_decay_bwd[l],
                       ret_gn_g[l], batch, seq, ctx_len)
    z_diff = _diff_attention(dq, dk, dv, dk_c, dv_c, diff_lambda[l], diff_gn_g[l], lam_init,
                             batch, seq, ctx_len)
    x1, f = _merge(x2, z_ret, z_diff, g_r, g_d, w_br_ret[l], w_br_diff[l], w_out[l],
                   post_mix_g[l], pre_ffn_g[l], ga_a, sh_f, sc_f, seq)

    fb, r2, n1, e1, e2 = _route(f, peer_w_q[l], peer_sub_keys[l], seq)
    out = _experts(fb, peer_u[l].astype(BF16), peer_v[l].T.astype(BF16), r2, n1, e1, e2, x1,
                   post_ffn_g[l], ga_f, seq)
    return out.reshape(batch, seq, d)
```

```python
import functools
import math

import jax
import jax.numpy as jnp
from jax import lax
from jax.experimental import pallas as pl
from jax.experimental.pallas import tpu as pltpu

F32 = jnp.float32
BF16 = jnp.bfloat16

D_MODEL = 1024
GRID_W = 64
EPS = 1e-6
ROPE_BASE = 10000.0

RET_HEADS = 4
RET_DK = 128
RET_DV = 256
RET_CHUNK = 128
RET_QK_W = RET_HEADS * RET_DK
RET_V_W = RET_HEADS * RET_DV

DIFF_HEADS = 8
DIFF_D = 64
DIFF_W = DIFF_HEADS * 2 * DIFF_D

PEER_HEADS = 8
PEER_NKEYS = 128
PEER_EXPERTS = PEER_NKEYS * PEER_NKEYS
PEER_HALF = 128
PEER_TOPK = 16

COL_RQ = 0
COL_RK = COL_RQ + RET_QK_W
COL_RV = COL_RK + RET_QK_W
COL_RG = COL_RV + RET_V_W
COL_DQ = COL_RG + RET_V_W
COL_DK = COL_DQ + DIFF_W
COL_DV = COL_DK + DIFF_W
COL_GR = COL_DV + DIFF_W
COL_GD = COL_GR + D_MODEL
IN_COLS = COL_GD + D_MODEL

LANES = 128
BF16_ROWS = 16
VMEM_LIMIT = 56 << 20

NEG_INF = float("-inf")
LOG2_E = math.log2(math.e)
RANK_NONE = 127.0

NT_DIMS = (((1,), (1,)), ((), ()))
TN_DIMS = (((0,), (0,)), ((), ()))


def _cparams(*sem, flags=None):
    return pltpu.CompilerParams(dimension_semantics=sem, vmem_limit_bytes=VMEM_LIMIT, flags=flags)


def _resident(shape):
    nd = len(shape)
    return pl.BlockSpec(shape, lambda *_: (0,) * nd, pipeline_mode=pl.Buffered(1))


def _mod_kernel(c_ref, w_ref, b_ref, o_ref):
    c = c_ref[...]
    s = c * jax.nn.sigmoid(c)
    o_ref[...] = jnp.dot(s, w_ref[...], preferred_element_type=F32,
                         precision=lax.Precision.HIGHEST) + b_ref[...]


def _modulation(cc, w, b):
    rows, d = cc.shape
    n = w.shape[1]
    tn = 768
    return pl.pallas_call(
        _mod_kernel,
        grid=(n // tn,),
        in_specs=[pl.BlockSpec((rows, d), lambda j: (0, 0)),
                  pl.BlockSpec((d, tn), lambda j: (0, j)),
                  pl.BlockSpec((1, tn), lambda j: (0, j))],
        out_specs=pl.BlockSpec((rows, tn), lambda j: (0, j)),
        out_shape=jax.ShapeDtypeStruct((rows, n), F32),
        compiler_params=_cparams("arbitrary"),
        name="modulation",
    )(cc, w, b.reshape(1, n))


def _rope_tables(seq, head_dim):
    rows = seq // GRID_W
    row = jnp.repeat(jnp.arange(rows, dtype=F32), GRID_W)
    col = jnp.tile(jnp.arange(GRID_W, dtype=F32), rows)
    half = head_dim // 2
    pair = half // 2
    lane = jnp.arange(LANES)
    d = lane % head_dim
    inv = ROPE_BASE ** (-jnp.arange(pair, dtype=F32) / pair)
    freq = inv[d % pair]
    pos = jnp.where((d < half)[None, :], row[:, None], col[:, None])
    ang = pos * freq[None, :]
    cos = jnp.cos(ang)
    sin = jnp.sin(ang)
    first = ((d % half) < pair)[None, :]
    sin_a = jnp.where(first, -sin, 0.0)
    sin_b = jnp.where(first, 0.0, sin)
    return cos, sin_a, sin_b, pair


def _rope(acc, cos, sin_a, sin_b, pair):
    up = pltpu.roll(acc, LANES - pair, 1)
    dn = pltpu.roll(acc, pair, 1)
    return acc * cos + up * sin_a + dn * sin_b


def _inproj_kernel(*refs, specs, rope, ret_pair, diff_pair):
    if rope:
        (x_ref, g_ref, sc_ref, sh_ref, w_ref,
         rc_ref, ra_ref, rb_ref, dc_ref, da_ref, db_ref) = refs[:11]
        out_refs = refs[11:]
    else:
        x_ref, g_ref, sc_ref, sh_ref, w_ref = refs[:5]
        out_refs = refs[5:]
    x = x_ref[...]
    y = x * lax.rsqrt(jnp.mean(x * x, axis=-1, keepdims=True) + EPS)
    u = (y * g_ref[...]) * (1.0 + sc_ref[0]) + sh_ref[0]
    ub = u.astype(BF16)
    for (col0, width, kind), o_ref in zip(specs, out_refs):
        for c in range(0, width, 512):
            cw = min(512, width - c)
            acc = jnp.dot(ub, w_ref[:, col0 + c:col0 + c + cw], preferred_element_type=F32)
            if kind in ("ret_q", "ret_k", "diff_q", "diff_k"):
                for l in range(0, cw, LANES):
                    a = acc[:, l:l + LANES]
                    if kind.startswith("ret"):
                        r = _rope(a, rc_ref[...], ra_ref[...], rb_ref[...], ret_pair)
                    else:
                        r = _rope(a, dc_ref[...], da_ref[...], db_ref[...], diff_pair)
                    if kind == "ret_q":
                        r = r * (RET_DK ** -0.5)
                    elif kind == "diff_q":
                        r = r * (DIFF_D ** -0.5 * LOG2_E)
                    o_ref[:, c + l:c + l + LANES] = r.astype(BF16)
            elif kind == "silu":
                o_ref[:, c:c + cw] = (acc * jax.nn.sigmoid(acc)).astype(BF16)
            elif kind == "sigmoid":
                o_ref[:, c:c + cw] = jax.nn.sigmoid(acc).astype(BF16)
            else:
                o_ref[:, c:c + cw] = acc.astype(BF16)


def _inproj(x2, gain, scale, shift, w_bf, specs, seq, tables):
    n, d = x2.shape
    tm = min(512, seq)
    assert seq % tm == 0 and n % seq == 0
    per_seq = seq // tm
    nb = scale.shape[0]
    if nb == 1:
        mod_map = lambda i: (0, 0, 0)
    else:
        mod_map = lambda i: (i // per_seq, 0, 0)
    rope = tables is not None
    in_specs = [pl.BlockSpec((tm, d), lambda i: (i, 0)),
                pl.BlockSpec((1, d), lambda i: (0, 0)),
                pl.BlockSpec((1, 1, d), mod_map),
                pl.BlockSpec((1, 1, d), mod_map),
                _resident(w_bf.shape)]
    args = [x2, gain.reshape(1, d), scale, shift, w_bf]
    ret_pair = diff_pair = 0
    if rope:
        (rc, ra, rb, ret_pair), (dc, da, db, diff_pair) = tables
        tab_spec = pl.BlockSpec((tm, LANES), lambda i: (i % per_seq, 0))
        in_specs += [tab_spec] * 6
        args += [rc, ra, rb, dc, da, db]
    out_specs = [pl.BlockSpec((tm, w), lambda i: (i, 0)) for (_, w, _) in specs]
    out_shape = [jax.ShapeDtypeStruct((n, w), BF16) for (_, w, _) in specs]
    kern = functools.partial(_inproj_kernel, specs=tuple(specs), rope=rope,
                             ret_pair=ret_pair, diff_pair=diff_pair)
    return pl.pallas_call(
        kern, grid=(n // tm,), in_specs=in_specs, out_specs=out_specs, out_shape=out_shape,
        compiler_params=_cparams("parallel"),
        name="inproj_rope" if rope else "inproj_ctx",
    )(*args)


def _ret_kernel(sdec_ref, q_ref, k_ref, v_ref, g_ref, kc_ref, vc_ref, m_ref, dec_ref, cdec_ref,
                gn_ref, o_ref, ybuf, sf, sb, *, n_chunks):
    C = RET_CHUNK
    h = pl.program_id(1)
    sdec_f = sdec_ref[2 * h]
    sdec_b = sdec_ref[2 * h + 1]

    kc = kc_ref[...].astype(F32)
    vc = vc_ref[...]
    sf[...] = lax.dot_general((kc * cdec_ref[0, 0]).astype(BF16), vc, TN_DIMS,
                              preferred_element_type=F32)
    sb[...] = lax.dot_general((kc * cdec_ref[0, 1]).astype(BF16), vc, TN_DIMS,
                              preferred_element_type=F32)

    dmat = m_ref[0]
    qdec_f = dec_ref[0, 0]
    kdec_f = dec_ref[0, 1]
    qdec_b = dec_ref[0, 2]
    kdec_b = dec_ref[0, 3]
    gn = gn_ref[...]

    def fwd(i, carry):
        r = pl.multiple_of(i * C, C)
        q = q_ref[pl.ds(r, C), :]
        k = k_ref[pl.ds(r, C), :]
        v = v_ref[pl.ds(r, C), :]
        qf = q.astype(F32)
        kf = k.astype(F32)
        att = lax.dot_general(q, k, NT_DIMS, preferred_element_type=F32) * dmat
        y = jnp.dot(att.astype(BF16), v, preferred_element_type=F32)
        y = y + jnp.dot((qf * qdec_f).astype(BF16), sf[...].astype(BF16),
                        preferred_element_type=F32)
        ybuf[pl.ds(r, C), :] = y
        sf[...] = sdec_f * sf[...] + lax.dot_general((kf * kdec_f).astype(BF16), v, TN_DIMS,
                                                     preferred_element_type=F32)
        return carry

    lax.fori_loop(0, n_chunks, fwd, 0)

    def bwd(j, carry):
        i = n_chunks - 1 - j
        r = pl.multiple_of(i * C, C)
        q = q_ref[pl.ds(r, C), :]
        k = k_ref[pl.ds(r, C), :]
        v = v_ref[pl.ds(r, C), :]
        qf = q.astype(F32)
        kf = k.astype(F32)
        y = ybuf[pl.ds(r, C), :] + jnp.dot((qf * qdec_b).astype(BF16), sb[...].astype(BF16),
                                           preferred_element_type=F32)
        sb[...] = sdec_b * sb[...] + lax.dot_general((kf * kdec_b).astype(BF16), v, TN_DIMS,
                                                     preferred_element_type=F32)
        mu = jnp.mean(y, axis=-1, keepdims=True)
        yc = y - mu
        var = jnp.mean(yc * yc, axis=-1, keepdims=True)
        yn = yc * lax.rsqrt(var + EPS) * gn
        o_ref[pl.ds(r, C), :] = (g_ref[pl.ds(r, C), :].astype(F32) * yn).astype(BF16)
        return carry

    lax.fori_loop(0, n_chunks, bwd, 0)


def _retention(ret_q, ret_k, ret_v, ret_g, ctx_rk, ctx_rv, dec_f, dec_b, gn_g, batch, seq, ctx_len):
    C = RET_CHUNK
    H = RET_HEADS
    lg_f = jnp.log1p(-jnp.exp2(dec_f.astype(F32)))
    lg_b = jnp.log1p(-jnp.exp2(dec_b.astype(F32)))
    idx = jnp.arange(C, dtype=F32)
    dist = idx[:, None] - idx[None, :]
    dmat = jnp.where(dist[None] >= 0,
                     jnp.exp(lg_f[:, None, None] * jnp.maximum(dist, 0.0)[None]),
                     jnp.exp(lg_b[:, None, None] * jnp.maximum(-dist, 0.0)[None]))
    qdec_f = jnp.exp(lg_f[:, None] * (idx + 1.0))
    kdec_f = jnp.exp(lg_f[:, None] * (C - 1.0 - idx))
    qdec_b = jnp.exp(lg_b[:, None] * (C - idx))
    kdec_b = jnp.exp(lg_b[:, None] * idx)
    dec = jnp.stack([qdec_f, kdec_f, qdec_b, kdec_b], axis=1)
    dec = jnp.broadcast_to(dec[..., None], (H, 4, C, RET_DK))
    cidx = jnp.arange(ctx_len, dtype=F32)
    cdec = jnp.stack([jnp.exp(lg_f[:, None] * (ctx_len - 1.0 - cidx)),
                      jnp.exp(lg_b[:, None] * cidx)], axis=1)
    cdec = jnp.broadcast_to(cdec[..., None], (H, 2, ctx_len, RET_DK))
    sdec = jnp.stack([jnp.exp(lg_f * C), jnp.exp(lg_b * C)], axis=1).reshape(2 * H)

    n_chunks = seq // C
    kern = functools.partial(_ret_kernel, n_chunks=n_chunks)
    return pl.pallas_call(
        kern,
        grid=(batch, H),
        in_specs=[pl.BlockSpec(memory_space=pltpu.SMEM),
                  pl.BlockSpec((seq, RET_DK), lambda b, h: (b, h)),
                  pl.BlockSpec((seq, RET_DK), lambda b, h: (b, h)),
                  pl.BlockSpec((seq, RET_DV), lambda b, h: (b, h)),
                  pl.BlockSpec((seq, RET_DV), lambda b, h: (b, h)),
                  pl.BlockSpec((ctx_len, RET_DK), lambda b, h: (b, h)),
                  pl.BlockSpec((ctx_len, RET_DV), lambda b, h: (b, h)),
                  pl.BlockSpec((1, C, C), lambda b, h: (h, 0, 0)),
                  pl.BlockSpec((1, 4, C, RET_DK), lambda b, h: (h, 0, 0, 0)),
                  pl.BlockSpec((1, 2, ctx_len, RET_DK), lambda b, h: (h, 0, 0, 0)),
                  pl.BlockSpec((1, RET_DV), lambda b, h: (0, h))],
        out_specs=pl.BlockSpec((seq, RET_DV), lambda b, h: (b, h)),
        out_shape=jax.ShapeDtypeStruct((batch * seq, RET_V_W), BF16),
        scratch_shapes=[pltpu.VMEM((seq, RET_DV), F32),
                        pltpu.VMEM((RET_DK, RET_DV), F32),
                        pltpu.VMEM((RET_DK, RET_DV), F32)],
        compiler_params=_cparams("parallel", "arbitrary"),
        name="retention",
    )(sdec, ret_q, ret_k, ret_v, ret_g, ctx_rk, ctx_rv, dmat, dec, cdec,
      gn_g.reshape(1, RET_V_W))


def _diff_kernel(q_ref, kl_ref, kc_ref, vl_ref, vc_ref, lp_ref, gn_ref, o_ref, *, lam_init):
    q = q_ref[...]
    lane = lax.broadcasted_iota(jnp.int32, q.shape, 1)
    zero = jnp.zeros_like(q)
    kl = kl_ref[...]
    kc = kc_ref[...]
    hw = q.shape[1]
    vl = jnp.concatenate([vl_ref[...], jnp.ones(vl_ref.shape, BF16)], axis=1)
    vc = jnp.concatenate([vc_ref[...], jnp.ones(vc_ref.shape, BF16)], axis=1)

    def branch(qm):
        sl = lax.dot_general(qm, kl, NT_DIMS, preferred_element_type=F32)
        sc = lax.dot_general(qm, kc, NT_DIMS, preferred_element_type=F32)
        m = jnp.maximum(jnp.max(sl, axis=-1, keepdims=True), jnp.max(sc, axis=-1, keepdims=True))
        pl_ = jnp.exp2(sl - m)
        pc = jnp.exp2(sc - m)
        o = (jnp.dot(pl_.astype(BF16), vl, preferred_element_type=F32)
             + jnp.dot(pc.astype(BF16), vc, preferred_element_type=F32))
        return o[:, :hw] / o[:, hw:]

    o1 = branch(jnp.where(lane < DIFF_D, q, zero))
    o2 = branch(jnp.where(lane >= DIFF_D, q, zero))
    lp = lp_ref[...]
    lam = (jnp.exp(jnp.sum(lp[0:1] * lp[1:2], axis=-1, keepdims=True))
           - jnp.exp(jnp.sum(lp[2:3] * lp[3:4], axis=-1, keepdims=True)) + lam_init)
    o = o1 - lam * o2
    on = o * lax.rsqrt(jnp.mean(o * o, axis=-1, keepdims=True) + EPS) * (1.0 - lam_init)
    o_ref[...] = (on * gn_ref[...]).astype(BF16)


def _diff_attention(dq, dk, dv, ctx_dk, ctx_dv, lam_params, gn_g, lam_init, batch, seq, ctx_len):
    tq = min(512, seq)
    nq = seq // tq
    hw = 2 * DIFF_D
    kern = functools.partial(_diff_kernel, lam_init=lam_init)
    return pl.pallas_call(
        kern,
        grid=(batch, DIFF_HEADS, nq),
        in_specs=[pl.BlockSpec((tq, hw), lambda b, h, i: (b * nq + i, h)),
                  pl.BlockSpec((seq, hw), lambda b, h, i: (b, h)),
                  pl.BlockSpec((ctx_len, hw), lambda b, h, i: (b, h)),
                  pl.BlockSpec((seq, hw), lambda b, h, i: (b, h)),
                  pl.BlockSpec((ctx_len, hw), lambda b, h, i: (b, h)),
                  pl.BlockSpec((4, DIFF_D), lambda b, h, i: (0, 0)),
                  pl.BlockSpec((1, hw), lambda b, h, i: (0, h))],
        out_specs=pl.BlockSpec((tq, hw), lambda b, h, i: (b * nq + i, h)),
        out_shape=jax.ShapeDtypeStruct((batch * seq, DIFF_W), BF16),
        compiler_params=_cparams("parallel", "parallel", "arbitrary"),
        name="diff_attention",
    )(dq, dk, ctx_dk, dv, ctx_dv, lam_params, gn_g.reshape(1, DIFF_W))


def _merge_kernel(x_ref, zr_ref, zd_ref, gr_ref, gd_ref, wr_ref, wd_ref, wo_ref,
                  pmg_ref, pfg_ref, ga_ref, shf_ref, scf_ref, x1_ref, f_ref):
    p_ret = jnp.dot(zr_ref[...], wr_ref[...], preferred_element_type=F32)
    p_diff = jnp.dot(zd_ref[...], wd_ref[...], preferred_element_type=F32)
    m = gr_ref[...].astype(F32) * p_ret + gd_ref[...].astype(F32) * p_diff
    mix = jnp.dot(m.astype(BF16), wo_ref[...], preferred_element_type=F32)
    mixn = mix * lax.rsqrt(jnp.mean(mix * mix, axis=-1, keepdims=True) + EPS) * pmg_ref[...]
    x1 = x_ref[...] + ga_ref[0] * mixn
    x1_ref[...] = x1
    fn = x1 * lax.rsqrt(jnp.mean(x1 * x1, axis=-1, keepdims=True) + EPS) * pfg_ref[...]
    f_ref[...] = fn * (1.0 + scf_ref[0]) + shf_ref[0]


def _merge(x2, z_ret, z_diff, g_r, g_d, w_br_ret, w_br_diff, w_out, post_mix_g, pre_ffn_g,
           ga_a, sh_f, sc_f, seq):
    n, d = x2.shape
    tm = min(512, seq)
    per_seq = seq // tm
    row = lambda i: (i, 0)
    mod = lambda i: (i // per_seq, 0, 0)
    vec = pl.BlockSpec((1, d), lambda i: (0, 0))
    return pl.pallas_call(
        _merge_kernel,
        grid=(n // tm,),
        in_specs=[pl.BlockSpec((tm, d), row)] * 5
                 + [_resident((d, d))] * 3
                 + [vec, vec]
                 + [pl.BlockSpec((1, 1, d), mod)] * 3,
        out_specs=[pl.BlockSpec((tm, d), row), pl.BlockSpec((tm, d), row)],
        out_shape=[jax.ShapeDtypeStruct((n, d), F32), jax.ShapeDtypeStruct((n, d), F32)],
        compiler_params=_cparams("parallel"),
        name="merge",
    )(x2, z_ret, z_diff, g_r, g_d, w_br_ret.astype(BF16), w_br_diff.astype(BF16),
      w_out.astype(BF16), post_mix_g.reshape(1, d), pre_ffn_g.reshape(1, d), ga_a, sh_f, sc_f)


def _staircase():
    return [(p, q) for p in range(PEER_TOPK) for q in range(PEER_TOPK)
            if (p + 1) * (q + 1) <= PEER_TOPK]


def _split_bf16(a):
    hi = a.astype(BF16)
    lo = (a - hi.astype(F32)).astype(BF16)
    return hi, lo


def _dot3(a_hi, a_lo, b_hi, b_lo, dims):
    d = lambda a, b: lax.dot_general(a, b, dims, preferred_element_type=F32)
    return d(a_hi, b_hi) + (d(a_hi, b_lo) + d(a_lo, b_hi))


def _bf16_pair_word(x):
    hi = pltpu.bitcast(x.astype(BF16).astype(F32), jnp.uint32)
    return hi | (hi >> 16)


def _bf16_rows(word_row):
    w = jnp.broadcast_to(word_row, (BF16_ROWS // 2, word_row.shape[1]))
    return pltpu.bitcast(w, BF16)


def _route_kernel(f_ref, wqh_ref, wql_ref, skh_ref, skl_ref, fb_ref, r2_ref, n1_ref, e1_ref, e2_ref,
                  s_scr, work, rank, top):
    K = PEER_NKEYS
    G = 2 * PEER_HEADS
    f = f_ref[...]
    fb_ref[...] = f.astype(BF16)
    f_hi, f_lo = _split_bf16(f)
    qt = _dot3(wqh_ref[...], wql_ref[...], f_hi, f_lo, NT_DIMS)
    for g in range(G):
        q_hi, q_lo = _split_bf16(qt[g * K:(g + 1) * K, :])
        s = _dot3(skh_ref[g], skl_ref[g], q_hi, q_lo, (((1,), (0,)), ((), ())))
        s_scr[g * K:(g + 1) * K, :] = s
        work[g * K:(g + 1) * K, :] = s
    rank[...] = jnp.full(rank.shape, RANK_NONE, F32)

    def extract(p, carry):
        pf = p.astype(F32)
        for g in range(G):
            hh, a = divmod(g, 2)
            w = work[g * K:(g + 1) * K, :]
            m = jnp.max(w, axis=0, keepdims=True)
            eq = w == m
            work[g * K:(g + 1) * K, :] = jnp.where(eq, NEG_INF, w)
            if a == 1:
                rk = rank[hh * K:(hh + 1) * K, :]
                rank[hh * K:(hh + 1) * K, :] = jnp.where(eq, pf, rk)
            top[pl.ds((a * PEER_TOPK + p) * PEER_HEADS + hh, 1), :] = m
        return carry

    lax.fori_loop(0, PEER_TOPK, extract, 0)

    H = PEER_HEADS
    tops_a = [top[p * H:(p + 1) * H, :] for p in range(PEER_TOPK)]
    tops_b = [top[(PEER_TOPK + q) * H:(PEER_TOPK + q + 1) * H, :] for q in range(PEER_TOPK)]
    pairs = _staircase()
    cand = [tops_a[p] + tops_b[q] for (p, q) in pairs]
    cur = list(cand)
    tau = None
    for it in range(PEER_TOPK):
        tau = functools.reduce(jnp.maximum, cur)
        if it + 1 < PEER_TOPK:
            cur = [jnp.where(c == tau, NEG_INF, c) for c in cur]
    c00 = cand[0]
    z = functools.reduce(
        lambda a, b: a + b,
        [jnp.where(c >= tau, jnp.exp(c - c00), 0.0) for c in cand])
    zinv = 1.0 / z

    for hh in range(H):
        s1 = s_scr[(2 * hh) * K:(2 * hh + 1) * K, :]
        s2 = s_scr[(2 * hh + 1) * K:(2 * hh + 2) * K, :]
        tau_h = tau[hh:hh + 1, :]
        cnt = jnp.zeros_like(s1)
        for q in range(PEER_TOPK):
            thr = tau_h - tops_b[q][hh:hh + 1, :]
            cnt = cnt + jnp.where(s1 >= thr, 1.0, 0.0)
        n1_ref[hh] = _bf16_pair_word(cnt)
        e1_ref[hh] = _bf16_pair_word(jnp.exp(s1 - tops_a[0][hh:hh + 1, :]) * zinv[hh:hh + 1, :])
        e2 = jnp.exp(s2 - tops_b[0][hh:hh + 1, :]).astype(BF16)
        e2_ref[hh] = pltpu.bitcast(e2, jnp.uint32)
        r2_ref[hh] = pltpu.bitcast(rank[hh * K:(hh + 1) * K, :].astype(BF16), jnp.uint32)


def _route(f, w_q, sub_keys, seq):
    n, d = f.shape
    tt = min(256, seq)
    H, K = PEER_HEADS, PEER_NKEYS
    wq_hi, wq_lo = _split_bf16(w_q.T)
    sk_hi, sk_lo = _split_bf16(sub_keys.reshape(2 * H, K, PEER_HALF))
    row_tab = jax.ShapeDtypeStruct((H, K, n), jnp.uint32)
    key_tab = jax.ShapeDtypeStruct((H, K // 2, n), jnp.uint32)
    row_spec = pl.BlockSpec((H, K, tt), lambda t: (0, 0, t))
    key_spec = pl.BlockSpec((H, K // 2, tt), lambda t: (0, 0, t))
    return pl.pallas_call(
        _route_kernel,
        grid=(n // tt,),
        in_specs=[pl.BlockSpec((tt, d), lambda t: (t, 0)),
                  _resident(wq_hi.shape), _resident(wq_lo.shape),
                  _resident(sk_hi.shape), _resident(sk_lo.shape)],
        out_specs=[pl.BlockSpec((tt, d), lambda t: (t, 0)), key_spec, row_spec, row_spec, key_spec],
        out_shape=[jax.ShapeDtypeStruct((n, d), BF16), key_tab, row_tab, row_tab, key_tab],
        scratch_shapes=[pltpu.VMEM((2 * H * K, tt), F32),
                        pltpu.VMEM((2 * H * K, tt), F32),
                        pltpu.VMEM((H * K, tt), F32),
                        pltpu.VMEM((2 * PEER_TOPK * H, tt), F32)],
        compiler_params=_cparams("parallel"),
        name="peer_route",
    )(f, wq_hi, wq_lo, sk_hi, sk_lo)


def _gelu(x):
    return 0.5 * x * (1.0 + lax.erf(x * (2.0 ** -0.5)))


def _expert_kernel(fb_ref, u_ref, vt_ref, r2_ref, e2_ref, n1_ref, e1_ref, x1_ref, g_ref, ga_ref,
                   o_ref, acc, h_scr, w_scr, *, rows_per_tile):
    e = pl.program_id(1)
    K = PEER_NKEYS

    @pl.when(e == 0)
    def _():
        acc[...] = jnp.zeros_like(acc)

    tt = fb_ref.shape[0]
    groups = K // BF16_ROWS
    zero = jnp.zeros((BF16_ROWS, tt), BF16)
    h_scr[...] = lax.dot_general(u_ref[...], fb_ref[...], NT_DIMS, preferred_element_type=F32)
    for i in range(rows_per_tile):
        gate = [None] * groups
        for hh in range(PEER_HEADS):
            n1 = _bf16_rows(n1_ref[hh, i:i + 1, :])
            e1 = _bf16_rows(e1_ref[hh, i:i + 1, :])
            for r in range(groups):
                words = slice(r * BF16_ROWS // 2, (r + 1) * BF16_ROWS // 2)
                r2 = pltpu.bitcast(r2_ref[hh, words, :], BF16)
                e2 = pltpu.bitcast(e2_ref[hh, words, :], BF16)
                term = jnp.where(r2 < n1, e1 * e2, zero)
                gate[r] = term if hh == 0 else gate[r] + term
        for r in range(groups):
            rows = slice(i * K + r * BF16_ROWS, i * K + (r + 1) * BF16_ROWS)
            w_scr[rows, :] = gate[r] * _gelu(h_scr[rows, :]).astype(BF16)
    acc[...] += jnp.dot(vt_ref[...], w_scr[...], preferred_element_type=F32)

    @pl.when(e == pl.num_programs(1) - 1)
    def _():
        y = acc[...]
        yn = y * lax.rsqrt(jnp.mean(y * y, axis=0, keepdims=True) + EPS)
        o_ref[...] = x1_ref[...] + ga_ref[0] * (yn.T * g_ref[...])


def _experts(fb, u_bf, vt_bf, r2, n1, e1, e2, x1, post_ffn_g, ga_f, seq):
    n, d = fb.shape
    tt = min(512, seq)
    per_seq = seq // tt
    rows_per_tile = 8
    et = rows_per_tile * PEER_NKEYS
    H, K = PEER_HEADS, PEER_NKEYS
    kern = functools.partial(_expert_kernel, rows_per_tile=rows_per_tile)
    return pl.pallas_call(
        kern,
        grid=(n // tt, PEER_EXPERTS // et),
        in_specs=[pl.BlockSpec((tt, d), lambda t, e: (t, 0)),
                  pl.BlockSpec((et, d), lambda t, e: (e, 0)),
                  pl.BlockSpec((d, et), lambda t, e: (0, e)),
                  pl.BlockSpec((H, K // 2, tt), lambda t, e: (0, 0, t)),
                  pl.BlockSpec((H, K // 2, tt), lambda t, e: (0, 0, t)),
                  pl.BlockSpec((H, rows_per_tile, tt), lambda t, e: (0, e, t)),
                  pl.BlockSpec((H, rows_per_tile, tt), lambda t, e: (0, e, t)),
                  pl.BlockSpec((tt, d), lambda t, e: (t, 0)),
                  pl.BlockSpec((1, d), lambda t, e: (0, 0)),
                  pl.BlockSpec((1, 1, d), lambda t, e: (t // per_seq, 0, 0))],
        out_specs=pl.BlockSpec((tt, d), lambda t, e: (t, 0)),
        out_shape=jax.ShapeDtypeStruct((n, d), F32),
        scratch_shapes=[pltpu.VMEM((d, tt), F32), pltpu.VMEM((et, tt), F32),
                        pltpu.VMEM((et, tt), BF16)],
        compiler_params=_cparams("parallel", "arbitrary"),
        name="peer_experts",
    )(fb, u_bf, vt_bf, r2, e2, n1, e1, x1, post_ffn_g.reshape(1, d), ga_f)


def kernel(x, c, ctx, c_ctx, w_mod, b_mod, pre_mix_g, post_mix_g, pre_ffn_g, post_ffn_g, w_in,
           ret_decay_fwd, ret_decay_bwd, ret_gn_g, diff_lambda, diff_gn_g, w_br_ret, w_br_diff,
           w_out, peer_w_q, peer_sub_keys, peer_u, peer_v):
    batch, seq, d = x.shape
    ctx_len = ctx.shape[1]
    depth = w_mod.shape[0]
    assert depth == 1 and d == D_MODEL
    l = 0
    lam_init = 0.8 - 0.6 * math.exp(-0.3 * l)

    rows = ((batch + 1 + 7) // 8) * 8
    cc = jnp.zeros((rows, d), F32).at[:batch].set(c).at[batch].set(c_ctx)
    mod = _modulation(cc, w_mod[l], b_mod[l])
    sh_a, sc_a, ga_a, sh_f, sc_f, ga_f = [t[:batch, None, :] for t in jnp.split(mod, 6, axis=-1)]
    csh_a, csc_a = [t[batch:batch + 1, None, :] for t in jnp.split(mod, 6, axis=-1)[:2]]

    w_in_bf = w_in[l].astype(BF16)
    x2 = x.reshape(batch * seq, d)
    ctx2 = ctx.reshape(batch * ctx_len, d)

    lat_specs = [(COL_RQ, RET_QK_W, "ret_q"), (COL_RK, RET_QK_W, "ret_k"),
                 (COL_RV, RET_V_W, "plain"), (COL_RG, RET_V_W, "silu"),
                 (COL_DQ, DIFF_W, "diff_q"), (COL_DK, DIFF_W, "diff_k"),
                 (COL_DV, DIFF_W, "plain"), (COL_GR, D_MODEL, "sigmoid"),
                 (COL_GD, D_MODEL, "sigmoid")]
    tables = (_rope_tables(seq, RET_DK), _rope_tables(seq, DIFF_D))
    rq, rk, rv, rg, dq, dk, dv, g_r, g_d = _inproj(
        x2, pre_mix_g[l], sc_a, sh_a, w_in_bf, lat_specs, seq, tables)

    ctx_specs = [(COL_RK, RET_QK_W, "plain"), (COL_RV, RET_V_W, "plain"),
                 (COL_DK, DIFF_W, "plain"), (COL_DV, DIFF_W, "plain")]
    rk_c, rv_c, dk_c, dv_c = _inproj(ctx2, pre_mix_g[l], csc_a, csh_a, w_in_bf, ctx_specs,
                                     ctx_len, None)

    z_ret = _retention(rq, rk, rv, rg, rk_c, rv_c, ret_decay_fwd[l], ret_decay_bwd[l],
                       ret_gn_g[l], batch, seq, ctx_len)
    z_diff = _diff_attention(dq, dk, dv, dk_c, dv_c, diff_lambda[l], diff_gn_g[l], lam_init,
                             batch, seq, ctx_len)
    x1, f = _merge(x2, z_ret, z_diff, g_r, g_d, w_br_ret[l], w_br_diff[l], w_out[l],
                   post_mix_g[l], pre_ffn_g[l], ga_a, sh_f, sc_f, seq)

    fb, r2, n1, e1, e2 = _route(f, peer_w_q[l], peer_sub_keys[l], seq)
    out = _experts(fb, peer_u[l].astype(BF16), peer_v[l].T.astype(BF16), r2, n1, e1, e2, x1,
                   post_ffn_g[l], ga_f, seq)
    return out.reshape(batch, seq, d)
```

```python
import functools
import math

import jax
import jax.numpy as jnp
from jax import lax
from jax.experimental import pallas as pl
from jax.experimental.pallas import tpu as pltpu

F32 = jnp.float32
BF16 = jnp.bfloat16

D_MODEL = 1024
GRID_W = 64
EPS = 1e-6
ROPE_BASE = 10000.0

RET_HEADS = 4
RET_DK = 128
RET_DV = 256
RET_CHUNK = 128
RET_UNROLL = 4
RET_QK_W = RET_HEADS * RET_DK
RET_V_W = RET_HEADS * RET_DV

DIFF_HEADS = 8
DIFF_D = 64
DIFF_W = DIFF_HEADS * 2 * DIFF_D
DIFF_SUBTILES = 4

PEER_HEADS = 8
PEER_NKEYS = 128
PEER_EXPERTS = PEER_NKEYS * PEER_NKEYS
PEER_HALF = 128
PEER_TOPK = 16

COL_RQ = 0
COL_RK = COL_RQ + RET_QK_W
COL_RV = COL_RK + RET_QK_W
COL_RG = COL_RV + RET_V_W
COL_DQ = COL_RG + RET_V_W
COL_DK = COL_DQ + DIFF_W
COL_DV = COL_DK + DIFF_W
COL_GR = COL_DV + DIFF_W
COL_GD = COL_GR + D_MODEL
IN_COLS = COL_GD + D_MODEL

LANES = 128
BF16_ROWS = 16
VMEM_LIMIT = 56 << 20

NEG_INF = float("-inf")
LOG2_E = math.log2(math.e)
RANK_NONE = 127.0

NT_DIMS = (((1,), (1,)), ((), ()))
TN_DIMS = (((0,), (0,)), ((), ()))


def _cparams(*sem, flags=None):
    return pltpu.CompilerParams(dimension_semantics=sem, vmem_limit_bytes=VMEM_LIMIT, flags=flags)


def _resident(shape):
    nd = len(shape)
    return pl.BlockSpec(shape, lambda *_: (0,) * nd, pipeline_mode=pl.Buffered(1))


def _mod_kernel(c_ref, w_ref, b_ref, o_ref):
    c = c_ref[...]
    s = c * jax.nn.sigmoid(c)
    o_ref[...] = jnp.dot(s, w_ref[...], preferred_element_type=F32,
                         precision=lax.Precision.HIGHEST) + b_ref[...]


def _modulation(cc, w, b):
    rows, d = cc.shape
    n = w.shape[1]
    tn = 768
    return pl.pallas_call(
        _mod_kernel,
        grid=(n // tn,),
        in_specs=[pl.BlockSpec((rows, d), lambda j: (0, 0)),
                  pl.BlockSpec((d, tn), lambda j: (0, j)),
                  pl.BlockSpec((1, tn), lambda j: (0, j))],
        out_specs=pl.BlockSpec((rows, tn), lambda j: (0, j)),
        out_shape=jax.ShapeDtypeStruct((rows, n), F32),
        compiler_params=_cparams("arbitrary"),
        name="modulation",
    )(cc, w, b.reshape(1, n))


def _rope_tables(seq, head_dim):
    rows = seq // GRID_W
    row = jnp.repeat(jnp.arange(rows, dtype=F32), GRID_W)
    col = jnp.tile(jnp.arange(GRID_W, dtype=F32), rows)
    half = head_dim // 2
    pair = half // 2
    lane = jnp.arange(LANES)
    d = lane % head_dim
    inv = ROPE_BASE ** (-jnp.arange(pair, dtype=F32) / pair)
    freq = inv[d % pair]
    pos = jnp.where((d < half)[None, :], row[:, None], col[:, None])
    ang = pos * freq[None, :]
    cos = jnp.cos(ang)
    sin = jnp.sin(ang)
    first = ((d % half) < pair)[None, :]
    sin_a = jnp.where(first, -sin, 0.0)
    sin_b = jnp.where(first, 0.0, sin)
    return cos, sin_a, sin_b, pair


def _rope(acc, cos, sin_a, sin_b, pair):
    up = pltpu.roll(acc, LANES - pair, 1)
    dn = pltpu.roll(acc, pair, 1)
    return acc * cos + up * sin_a + dn * sin_b


def _inproj_kernel(*refs, specs, rope, ret_pair, diff_pair):
    if rope:
        (x_ref, g_ref, sc_ref, sh_ref, w_ref,
         rc_ref, ra_ref, rb_ref, dc_ref, da_ref, db_ref) = refs[:11]
        out_refs = refs[11:]
    else:
        x_ref, g_ref, sc_ref, sh_ref, w_ref = refs[:5]
        out_refs = refs[5:]
    x = x_ref[...]
    y = x * lax.rsqrt(jnp.mean(x * x, axis=-1, keepdims=True) + EPS)
    u = (y * g_ref[...]) * (1.0 + sc_ref[0]) + sh_ref[0]
    ub = u.astype(BF16)
    for (col0, width, kind), o_ref in zip(specs, out_refs):
        for c in range(0, width, 512):
            cw = min(512, width - c)
            acc = jnp.dot(ub, w_ref[:, col0 + c:col0 + c + cw], preferred_element_type=F32)
            if kind in ("ret_q", "ret_k", "diff_q", "diff_k"):
                for l in range(0, cw, LANES):
                    a = acc[:, l:l + LANES]
                    if kind.startswith("ret"):
                        r = _rope(a, rc_ref[...], ra_ref[...], rb_ref[...], ret_pair)
                    else:
                        r = _rope(a, dc_ref[...], da_ref[...], db_ref[...], diff_pair)
                    if kind == "ret_q":
                        r = r * (RET_DK ** -0.5)
                    elif kind == "diff_q":
                        r = r * (DIFF_D ** -0.5 * LOG2_E)
                    o_ref[:, c + l:c + l + LANES] = r.astype(BF16)
            elif kind == "silu":
                o_ref[:, c:c + cw] = (acc * jax.nn.sigmoid(acc)).astype(BF16)
            elif kind == "sigmoid":
                o_ref[:, c:c + cw] = jax.nn.sigmoid(acc).astype(BF16)
            else:
                o_ref[:, c:c + cw] = acc.astype(BF16)


def _inproj(x2, gain, scale, shift, w_bf, specs, seq, tables):
    n, d = x2.shape
    tm = min(512, seq)
    assert seq % tm == 0 and n % seq == 0
    per_seq = seq // tm
    nb = scale.shape[0]
    if nb == 1:
        mod_map = lambda i: (0, 0, 0)
    else:
        mod_map = lambda i: (i // per_seq, 0, 0)
    rope = tables is not None
    in_specs = [pl.BlockSpec((tm, d), lambda i: (i, 0)),
                pl.BlockSpec((1, d), lambda i: (0, 0)),
                pl.BlockSpec((1, 1, d), mod_map),
                pl.BlockSpec((1, 1, d), mod_map),
                _resident(w_bf.shape)]
    args = [x2, gain.reshape(1, d), scale, shift, w_bf]
    ret_pair = diff_pair = 0
    if rope:
        (rc, ra, rb, ret_pair), (dc, da, db, diff_pair) = tables
        tab_spec = pl.BlockSpec((tm, LANES), lambda i: (i % per_seq, 0))
        in_specs += [tab_spec] * 6
        args += [rc, ra, rb, dc, da, db]
    out_specs = [pl.BlockSpec((tm, w), lambda i: (i, 0)) for (_, w, _) in specs]
    out_shape = [jax.ShapeDtypeStruct((n, w), BF16) for (_, w, _) in specs]
    kern = functools.partial(_inproj_kernel, specs=tuple(specs), rope=rope,
                             ret_pair=ret_pair, diff_pair=diff_pair)
    return pl.pallas_call(
        kern, grid=(n // tm,), in_specs=in_specs, out_specs=out_specs, out_shape=out_shape,
        compiler_params=_cparams("parallel"),
        name="inproj_rope" if rope else "inproj_ctx",
    )(*args)


def _ret_kernel(sdec_ref, q_ref, k_ref, v_ref, g_ref, kc_ref, vc_ref, m_ref, dec_ref, cdec_ref,
                gn_ref, o_ref, ybuf, kv, st, sf, sb, *, n_chunks):
    C = RET_CHUNK
    DK = RET_DK
    h = pl.program_id(1)
    sdec_f = sdec_ref[2 * h]
    sdec_b = sdec_ref[2 * h + 1]

    kc = kc_ref[...].astype(F32)
    vc = vc_ref[...]
    sf[...] = lax.dot_general((kc * cdec_ref[0, 0]).astype(BF16), vc, TN_DIMS,
                              preferred_element_type=F32)
    sb[...] = lax.dot_general((kc * cdec_ref[0, 1]).astype(BF16), vc, TN_DIMS,
                              preferred_element_type=F32)

    dmat = m_ref[0]
    qdec_f = dec_ref[0, 0]
    kdec_f = dec_ref[0, 1]
    qdec_b = dec_ref[0, 2]
    kdec_b = dec_ref[0, 3]
    gn = gn_ref[...]

    def intra(i, carry):
        r = pl.multiple_of(i * C, C)
        q = q_ref[pl.ds(r, C), :]
        k = k_ref[pl.ds(r, C), :]
        v = v_ref[pl.ds(r, C), :]
        kf = k.astype(F32)
        att = lax.dot_general(q, k, NT_DIMS, preferred_element_type=F32) * dmat
        ybuf[pl.ds(r, C), :] = jnp.dot(att.astype(BF16), v, preferred_element_type=F32)
        kk = jnp.concatenate([(kf * kdec_f).astype(BF16), (kf * kdec_b).astype(BF16)], axis=1)
        kv[i] = lax.dot_general(kk, v, TN_DIMS, preferred_element_type=F32)
        return carry

    lax.fori_loop(0, n_chunks, intra, 0, unroll=RET_UNROLL)

    def scan(t, carry):
        i = t
        st[i, :DK, :] = sf[...].astype(BF16)
        sf[...] = sdec_f * sf[...] + kv[i, :DK, :]
        j = n_chunks - 1 - t
        st[j, DK:, :] = sb[...].astype(BF16)
        sb[...] = sdec_b * sb[...] + kv[j, DK:, :]
        return carry

    lax.fori_loop(0, n_chunks, scan, 0)

    def cross(i, carry):
        r = pl.multiple_of(i * C, C)
        qf = q_ref[pl.ds(r, C), :].astype(F32)
        qq = jnp.concatenate([(qf * qdec_f).astype(BF16), (qf * qdec_b).astype(BF16)], axis=1)
        y = ybuf[pl.ds(r, C), :] + jnp.dot(qq, st[i], preferred_element_type=F32)
        mu = jnp.mean(y, axis=-1, keepdims=True)
        yc = y - mu
        var = jnp.mean(yc * yc, axis=-1, keepdims=True)
        yn = yc * lax.rsqrt(var + EPS) * gn
        o_ref[pl.ds(r, C), :] = (g_ref[pl.ds(r, C), :].astype(F32) * yn).astype(BF16)
        return carry

    lax.fori_loop(0, n_chunks, cross, 0, unroll=RET_UNROLL)


def _retention(ret_q, ret_k, ret_v, ret_g, ctx_rk, ctx_rv, dec_f, dec_b, gn_g, batch, seq, ctx_len):
    C = RET_CHUNK
    H = RET_HEADS
    lg_f = jnp.log1p(-jnp.exp2(dec_f.astype(F32)))
    lg_b = jnp.log1p(-jnp.exp2(dec_b.astype(F32)))
    idx = jnp.arange(C, dtype=F32)
    dist = idx[:, None] - idx[None, :]
    dmat = jnp.where(dist[None] >= 0,
                     jnp.exp(lg_f[:, None, None] * jnp.maximum(dist, 0.0)[None]),
                     jnp.exp(lg_b[:, None, None] * jnp.maximum(-dist, 0.0)[None]))
    qdec_f = jnp.exp(lg_f[:, None] * (idx + 1.0))
    kdec_f = jnp.exp(lg_f[:, None] * (C - 1.0 - idx))
    qdec_b = jnp.exp(lg_b[:, None] * (C - idx))
    kdec_b = jnp.exp(lg_b[:, None] * idx)
    dec = jnp.stack([qdec_f, kdec_f, qdec_b, kdec_b], axis=1)
    dec = jnp.broadcast_to(dec[..., None], (H, 4, C, RET_DK))
    cidx = jnp.arange(ctx_len, dtype=F32)
    cdec = jnp.stack([jnp.exp(lg_f[:, None] * (ctx_len - 1.0 - cidx)),
                      jnp.exp(lg_b[:, None] * cidx)], axis=1)
    cdec = jnp.broadcast_to(cdec[..., None], (H, 2, ctx_len, RET_DK))
    sdec = jnp.stack([jnp.exp(lg_f * C), jnp.exp(lg_b * C)], axis=1).reshape(2 * H)

    n_chunks = seq // C
    kern = functools.partial(_ret_kernel, n_chunks=n_chunks)
    return pl.pallas_call(
        kern,
        grid=(batch, H),
        in_specs=[pl.BlockSpec(memory_space=pltpu.SMEM),
                  pl.BlockSpec((seq, RET_DK), lambda b, h: (b, h)),
                  pl.BlockSpec((seq, RET_DK), lambda b, h: (b, h)),
                  pl.BlockSpec((seq, RET_DV), lambda b, h: (b, h)),
                  pl.BlockSpec((seq, RET_DV), lambda b, h: (b, h)),
                  pl.BlockSpec((ctx_len, RET_DK), lambda b, h: (b, h)),
                  pl.BlockSpec((ctx_len, RET_DV), lambda b, h: (b, h)),
                  pl.BlockSpec((1, C, C), lambda b, h: (h, 0, 0)),
                  pl.BlockSpec((1, 4, C, RET_DK), lambda b, h: (h, 0, 0, 0)),
                  pl.BlockSpec((1, 2, ctx_len, RET_DK), lambda b, h: (h, 0, 0, 0)),
                  pl.BlockSpec((1, RET_DV), lambda b, h: (0, h))],
        out_specs=pl.BlockSpec((seq, RET_DV), lambda b, h: (b, h)),
        out_shape=jax.ShapeDtypeStruct((batch * seq, RET_V_W), BF16),
        scratch_shapes=[pltpu.VMEM((seq, RET_DV), F32),
                        pltpu.VMEM((n_chunks, 2 * RET_DK, RET_DV), F32),
                        pltpu.VMEM((n_chunks, 2 * RET_DK, RET_DV), BF16),
                        pltpu.VMEM((RET_DK, RET_DV), F32),
                        pltpu.VMEM((RET_DK, RET_DV), F32)],
        compiler_params=_cparams("parallel", "arbitrary"),
        name="retention",
    )(sdec, ret_q, ret_k, ret_v, ret_g, ctx_rk, ctx_rv, dmat, dec, cdec,
      gn_g.reshape(1, RET_V_W))


def _diff_kernel(q_ref, kl_ref, kc_ref, vl_ref, vc_ref, lp_ref, gn_ref, o_ref, *, lam_init):
    kl = kl_ref[...]
    kc = kc_ref[...]
    tq, hw = q_ref.shape
    vl = jnp.concatenate([vl_ref[...], jnp.ones(vl_ref.shape, BF16)], axis=1)
    vc = jnp.concatenate([vc_ref[...], jnp.ones(vc_ref.shape, BF16)], axis=1)
    lp = lp_ref[...]
    lam = (jnp.exp(jnp.sum(lp[0:1] * lp[1:2], axis=-1, keepdims=True))
           - jnp.exp(jnp.sum(lp[2:3] * lp[3:4], axis=-1, keepdims=True)) + lam_init)

    def branch(qm):
        sl = lax.dot_general(qm, kl, NT_DIMS, preferred_element_type=F32)
        sc = lax.dot_general(qm, kc, NT_DIMS, preferred_element_type=F32)
        m = jnp.maximum(jnp.max(sl, axis=-1, keepdims=True), jnp.max(sc, axis=-1, keepdims=True))
        pl_ = jnp.exp2(sl - m)
        pc = jnp.exp2(sc - m)
        o = (jnp.dot(pl_.astype(BF16), vl, preferred_element_type=F32)
             + jnp.dot(pc.astype(BF16), vc, preferred_element_type=F32))
        return o[:, :hw] / o[:, hw:]

    sub = tq // DIFF_SUBTILES
    for t in range(DIFF_SUBTILES):
        rows = slice(t * sub, (t + 1) * sub)
        q = q_ref[rows, :]
        lane = lax.broadcasted_iota(jnp.int32, q.shape, 1)
        zero = jnp.zeros_like(q)
        o1 = branch(jnp.where(lane < DIFF_D, q, zero))
        o2 = branch(jnp.where(lane >= DIFF_D, q, zero))
        o = o1 - lam * o2
        on = o * lax.rsqrt(jnp.mean(o * o, axis=-1, keepdims=True) + EPS) * (1.0 - lam_init)
        o_ref[rows, :] = (on * gn_ref[...]).astype(BF16)


def _diff_attention(dq, dk, dv, ctx_dk, ctx_dv, lam_params, gn_g, lam_init, batch, seq, ctx_len):
    tq = min(512, seq)
    nq = seq // tq
    hw = 2 * DIFF_D
    kern = functools.partial(_diff_kernel, lam_init=lam_init)
    return pl.pallas_call(
        kern,
        grid=(batch, DIFF_HEADS, nq),
        in_specs=[pl.BlockSpec((tq, hw), lambda b, h, i: (b * nq + i, h)),
                  pl.BlockSpec((seq, hw), lambda b, h, i: (b, h)),
                  pl.BlockSpec((ctx_len, hw), lambda b, h, i: (b, h)),
                  pl.BlockSpec((seq, hw), lambda b, h, i: (b, h)),
                  pl.BlockSpec((ctx_len, hw), lambda b, h, i: (b, h)),
                  pl.BlockSpec((4, DIFF_D), lambda b, h, i: (0, 0)),
                  pl.BlockSpec((1, hw), lambda b, h, i: (0, h))],
        out_specs=pl.BlockSpec((tq, hw), lambda b, h, i: (b * nq + i, h)),
        out_shape=jax.ShapeDtypeStruct((batch * seq, DIFF_W), BF16),
        compiler_params=_cparams("parallel", "parallel", "arbitrary"),
        name="diff_attention",
    )(dq, dk, ctx_dk, dv, ctx_dv, lam_params, gn_g.reshape(1, DIFF_W))


def _merge_kernel(x_ref, zr_ref, zd_ref, gr_ref, gd_ref, wr_ref, wd_ref, wo_ref,
                  pmg_ref, pfg_ref, ga_ref, shf_ref, scf_ref, x1_ref, f_ref):
    p_ret = jnp.dot(zr_ref[...], wr_ref[...], preferred_element_type=F32)
    p_diff = jnp.dot(zd_ref[...], wd_ref[...], preferred_element_type=F32)
    m = gr_ref[...].astype(F32) * p_ret + gd_ref[...].astype(F32) * p_diff
    mix = jnp.dot(m.astype(BF16), wo_ref[...], preferred_element_type=F32)
    mixn = mix * lax.rsqrt(jnp.mean(mix * mix, axis=-1, keepdims=True) + EPS) * pmg_ref[...]
    x1 = x_ref[...] + ga_ref[0] * mixn
    x1_ref[...] = x1
    fn = x1 * lax.rsqrt(jnp.mean(x1 * x1, axis=-1, keepdims=True) + EPS) * pfg_ref[...]
    f_ref[...] = (fn * (1.0 + scf_ref[0]) + shf_ref[0]).astype(BF16)


def _merge(x2, z_ret, z_diff, g_r, g_d, w_br_ret, w_br_diff, w_out, post_mix_g, pre_ffn_g,
           ga_a, sh_f, sc_f, seq):
    n, d = x2.shape
    tm = min(512, seq)
    per_seq = seq // tm
    row = lambda i: (i, 0)
    mod = lambda i: (i // per_seq, 0, 0)
    vec = pl.BlockSpec((1, d), lambda i: (0, 0))
    return pl.pallas_call(
        _merge_kernel,
        grid=(n // tm,),
        in_specs=[pl.BlockSpec((tm, d), row)] * 5
                 + [_resident((d, d))] * 3
                 + [vec, vec]
                 + [pl.BlockSpec((1, 1, d), mod)] * 3,
        out_specs=[pl.BlockSpec((tm, d), row), pl.BlockSpec((tm, d), row)],
        out_shape=[jax.ShapeDtypeStruct((n, d), F32), jax.ShapeDtypeStruct((n, d), BF16)],
        compiler_params=_cparams("parallel"),
        name="merge",
    )(x2, z_ret, z_diff, g_r, g_d, w_br_ret.astype(BF16), w_br_diff.astype(BF16),
      w_out.astype(BF16), post_mix_g.reshape(1, d), pre_ffn_g.reshape(1, d), ga_a, sh_f, sc_f)


def _staircase():
    return [(p, q) for p in range(PEER_TOPK) for q in range(PEER_TOPK)
            if (p + 1) * (q + 1) <= PEER_TOPK]


def _bf16_pair_word(x):
    hi = pltpu.bitcast(x.astype(BF16).astype(F32), jnp.uint32)
    return hi | (hi >> 16)


def _bf16_rows(word_row):
    w = jnp.broadcast_to(word_row, (BF16_ROWS // 2, word_row.shape[1]))
    return pltpu.bitcast(w, BF16)


def _count_leading(x, thr):
    assert len(thr) == 16
    count = jnp.zeros_like(x)
    decisions = []
    for width in (8, 4, 2, 1):
        cands = [thr[base + width - 1] for base in range(0, 16, 2 * width)]
        for bit in reversed(decisions):
            cands = [jnp.where(bit, hi, lo) for lo, hi in zip(cands[0::2], cands[1::2])]
        passed = x >= cands[0]
        decisions.append(passed)
        count = count + jnp.where(passed, float(width), 0.0)
    return jnp.where(x >= thr[15], 16.0, count)


def _route_kernel(fb_ref, wq_ref, sk_ref, r2_ref, n1_ref, e1_ref, e2_ref,
                  s_scr, work, rank, top):
    K = PEER_NKEYS
    G = 2 * PEER_HEADS
    qt = lax.dot_general(wq_ref[...], fb_ref[...], NT_DIMS, preferred_element_type=F32)
    for g in range(G):
        s = jnp.dot(sk_ref[g], qt[g * K:(g + 1) * K, :].astype(BF16), preferred_element_type=F32)
        s_scr[g * K:(g + 1) * K, :] = s
        work[g * K:(g + 1) * K, :] = s
    rank[...] = jnp.full(rank.shape, RANK_NONE, F32)

    def extract(p, carry):
        pf = p.astype(F32)
        for g in range(G):
            hh, a = divmod(g, 2)
            w = work[g * K:(g + 1) * K, :]
            m = jnp.max(w, axis=0, keepdims=True)
            eq = w == m
            work[g * K:(g + 1) * K, :] = jnp.where(eq, NEG_INF, w)
            if a == 1:
                rk = rank[hh * K:(hh + 1) * K, :]
                rank[hh * K:(hh + 1) * K, :] = jnp.where(eq, pf, rk)
            top[pl.ds((a * PEER_TOPK + p) * PEER_HEADS + hh, 1), :] = m
        return carry

    lax.fori_loop(0, PEER_TOPK, extract, 0)

    H = PEER_HEADS
    tops_a = [top[p * H:(p + 1) * H, :] for p in range(PEER_TOPK)]
    tops_b = [top[(PEER_TOPK + q) * H:(PEER_TOPK + q + 1) * H, :] for q in range(PEER_TOPK)]
    pairs = _staircase()
    cand = [tops_a[p] + tops_b[q] for (p, q) in pairs]
    cur = list(cand)
    tau = None
    for it in range(PEER_TOPK):
        tau = functools.reduce(jnp.maximum, cur)
        if it + 1 < PEER_TOPK:
            cur = [jnp.where(c == tau, NEG_INF, c) for c in cur]
    c00 = cand[0]
    z = functools.reduce(
        lambda a, b: a + b,
        [jnp.where(c >= tau, jnp.exp(c - c00), 0.0) for c in cand])
    zinv = 1.0 / z

    for hh in range(H):
        s1 = s_scr[(2 * hh) * K:(2 * hh + 1) * K, :]
        s2 = s_scr[(2 * hh + 1) * K:(2 * hh + 2) * K, :]
        tau_h = tau[hh:hh + 1, :]
        thr = [tau_h - tops_b[q][hh:hh + 1, :] for q in range(PEER_TOPK)]
        cnt = _count_leading(s1, thr)
        n1_ref[hh] = _bf16_pair_word(cnt)
        e1_ref[hh] = _bf16_pair_word(jnp.exp(s1 - tops_a[0][hh:hh + 1, :]) * zinv[hh:hh + 1, :])
        e2 = jnp.exp(s2 - tops_b[0][hh:hh + 1, :]).astype(BF16)
        e2_ref[hh] = pltpu.bitcast(e2, jnp.uint32)
        r2_ref[hh] = pltpu.bitcast(rank[hh * K:(hh + 1) * K, :].astype(BF16), jnp.uint32)


def _route(f, w_q, sub_keys, seq):
    n, d = f.shape
    tt = min(256, seq)
    H, K = PEER_HEADS, PEER_NKEYS
    wq_t = w_q.T.astype(BF16)
    sk = sub_keys.reshape(2 * H, K, PEER_HALF).astype(BF16)
    row_tab = jax.ShapeDtypeStruct((H, K, n), jnp.uint32)
    key_tab = jax.ShapeDtypeStruct((H, K // 2, n), jnp.uint32)
    row_spec = pl.BlockSpec((H, K, tt), lambda t: (0, 0, t))
    key_spec = pl.BlockSpec((H, K // 2, tt), lambda t: (0, 0, t))
    return pl.pallas_call(
        _route_kernel,
        grid=(n // tt,),
        in_specs=[pl.BlockSpec((tt, d), lambda t: (t, 0)),
                  _resident(wq_t.shape), _resident(sk.shape)],
        out_specs=[key_spec, row_spec, row_spec, key_spec],
        out_shape=[key_tab, row_tab, row_tab, key_tab],
        scratch_shapes=[pltpu.VMEM((2 * H * K, tt), F32),
                        pltpu.VMEM((2 * H * K, tt), F32),
                        pltpu.VMEM((H * K, tt), F32),
                        pltpu.VMEM((2 * PEER_TOPK * H, tt), F32)],
        compiler_params=_cparams("parallel"),
        name="peer_route",
    )(f, wq_t, sk)


def _gelu(x):
    return 0.5 * x * (1.0 + lax.erf(x * (2.0 ** -0.5)))


def _expert_kernel(fb_ref, u_ref, vt_ref, r2_ref, e2_ref, n1_ref, e1_ref, x1_ref, g_ref, ga_ref,
                   o_ref, acc, h_scr, w_scr, *, rows_per_tile):
    e = pl.program_id(1)
    K = PEER_NKEYS

    @pl.when(e == 0)
    def _():
        acc[...] = jnp.zeros_like(acc)

    tt = fb_ref.shape[0]
    groups = K // BF16_ROWS
    zero = jnp.zeros((BF16_ROWS, tt), BF16)
    h_scr[...] = lax.dot_general(u_ref[...], fb_ref[...], NT_DIMS, preferred_element_type=F32)
    for i in range(rows_per_tile):
        gate = [None] * groups
        for hh in range(PEER_HEADS):
            n1 = _bf16_rows(n1_ref[hh, i:i + 1, :])
            e1 = _bf16_rows(e1_ref[hh, i:i + 1, :])
            for r in range(groups):
                words = slice(r * BF16_ROWS // 2, (r + 1) * BF16_ROWS // 2)
                r2 = pltpu.bitcast(r2_ref[hh, words, :], BF16)
                e2 = pltpu.bitcast(e2_ref[hh, words, :], BF16)
                term = jnp.where(r2 < n1, e1 * e2, zero)
                gate[r] = term if hh == 0 else gate[r] + term
        for r in range(groups):
            rows = slice(i * K + r * BF16_ROWS, i * K + (r + 1) * BF16_ROWS)
            w_scr[rows, :] = gate[r] * _gelu(h_scr[rows, :]).astype(BF16)
    acc[...] += jnp.dot(vt_ref[...], w_scr[...], preferred_element_type=F32)

    @pl.when(e == pl.num_programs(1) - 1)
    def _():
        y = acc[...]
        yn = y * lax.rsqrt(jnp.mean(y * y, axis=0, keepdims=True) + EPS)
        o_ref[...] = x1_ref[...] + ga_ref[0] * (yn.T * g_ref[...])


def _experts(fb, u_bf, vt_bf, r2, n1, e1, e2, x1, post_ffn_g, ga_f, seq):
    n, d = fb.shape
    tt = min(512, seq)
    per_seq = seq // tt
    rows_per_tile = 8
    et = rows_per_tile * PEER_NKEYS
    H, K = PEER_HEADS, PEER_NKEYS
    kern = functools.partial(_expert_kernel, rows_per_tile=rows_per_tile)
    return pl.pallas_call(
        kern,
        grid=(n // tt, PEER_EXPERTS // et),
        in_specs=[pl.BlockSpec((tt, d), lambda t, e: (t, 0)),
                  pl.BlockSpec((et, d), lambda t, e: (e, 0)),
                  pl.BlockSpec((d, et), lambda t, e: (0, e)),
                  pl.BlockSpec((H, K // 2, tt), lambda t, e: (0, 0, t)),
                  pl.BlockSpec((H, K // 2, tt), lambda t, e: (0, 0, t)),
                  pl.BlockSpec((H, rows_per_tile, tt), lambda t, e: (0, e, t)),
                  pl.BlockSpec((H, rows_per_tile, tt), lambda t, e: (0, e, t)),
                  pl.BlockSpec((tt, d), lambda t, e: (t, 0)),
                  pl.BlockSpec((1, d), lambda t, e: (0, 0)),
                  pl.BlockSpec((1, 1, d), lambda t, e: (t // per_seq, 0, 0))],
        out_specs=pl.BlockSpec((tt, d), lambda t, e: (t, 0)),
        out_shape=jax.ShapeDtypeStruct((n, d), F32),
        scratch_shapes=[pltpu.VMEM((d, tt), F32), pltpu.VMEM((et, tt), F32),
                        pltpu.VMEM((et, tt), BF16)],
        compiler_params=_cparams("parallel", "arbitrary"),
        name="peer_experts",
    )(fb, u_bf, vt_bf, r2, e2, n1, e1, x1, post_ffn_g.reshape(1, d), ga_f)


def kernel(x, c, ctx, c_ctx, w_mod, b_mod, pre_mix_g, post_mix_g, pre_ffn_g, post_ffn_g, w_in,
           ret_decay_fwd, ret_decay_bwd, ret_gn_g, diff_lambda, diff_gn_g, w_br_ret, w_br_diff,
           w_out, peer_w_q, peer_sub_keys, peer_u, peer_v):
    batch, seq, d = x.shape
    ctx_len = ctx.shape[1]
    depth = w_mod.shape[0]
    assert depth == 1 and d == D_MODEL
    l = 0
    lam_init = 0.8 - 0.6 * math.exp(-0.3 * l)

    rows = ((batch + 1 + 7) // 8) * 8
    cc = jnp.zeros((rows, d), F32).at[:batch].set(c).at[batch].set(c_ctx)
    mod = _modulation(cc, w_mod[l], b_mod[l])
    sh_a, sc_a, ga_a, sh_f, sc_f, ga_f = [t[:batch, None, :] for t in jnp.split(mod, 6, axis=-1)]
    csh_a, csc_a = [t[batch:batch + 1, None, :] for t in jnp.split(mod, 6, axis=-1)[:2]]

    w_in_bf = w_in[l].astype(BF16)
    x2 = x.reshape(batch * seq, d)
    ctx2 = ctx.reshape(batch * ctx_len, d)

    lat_specs = [(COL_RQ, RET_QK_W, "ret_q"), (COL_RK, RET_QK_W, "ret_k"),
                 (COL_RV, RET_V_W, "plain"), (COL_RG, RET_V_W, "silu"),
                 (COL_DQ, DIFF_W, "diff_q"), (COL_DK, DIFF_W, "diff_k"),
                 (COL_DV, DIFF_W, "plain"), (COL_GR, D_MODEL, "sigmoid"),
                 (COL_GD, D_MODEL, "sigmoid")]
    tables = (_rope_tables(seq, RET_DK), _rope_tables(seq, DIFF_D))
    rq, rk, rv, rg, dq, dk, dv, g_r, g_d = _inproj(
        x2, pre_mix_g[l], sc_a, sh_a, w_in_bf, lat_specs, seq, tables)

    ctx_specs = [(COL_RK, RET_QK_W, "plain"), (COL_RV, RET_V_W, "plain"),
                 (COL_DK, DIFF_W, "plain"), (COL_DV, DIFF_W, "plain")]
    rk_c, rv_c, dk_c, dv_c = _inproj(ctx2, pre_mix_g[l], csc_a, csh_a, w_in_bf, ctx_specs,
                                     ctx_len, None)

    z_ret = _retention(rq, rk, rv, rg, rk_c, rv_c, ret_decay_fwd[l], ret_decay_bwd[l],
                       ret_gn_g[l], batch, seq, ctx_len)
    z_diff = _diff_attention(dq, dk, dv, dk_c, dv_c, diff_lambda[l], diff_gn_g[l], lam_init,
                             batch, seq, ctx_len)
    x1, fb = _merge(x2, z_ret, z_diff, g_r, g_d, w_br_ret[l], w_br_diff[l], w_out[l],
                   post_mix_g[l], pre_ffn_g[l], ga_a, sh_f, sc_f, seq)

    r2, n1, e1, e2 = _route(fb, peer_w_q[l], peer_sub_keys[l], seq)
    out = _experts(fb, peer_u[l].astype(BF16), peer_v[l].T.astype(BF16), r2, n1, e1, e2, x1,
                   post_ffn_g[l], ga_f, seq)
    return out.reshape(batch, seq, d)
```

```python
import functools
import math

import jax
import jax.numpy as jnp
from jax import lax
from jax.experimental import pallas as pl
from jax.experimental.pallas import tpu as pltpu

F32 = jnp.float32
BF16 = jnp.bfloat16

D_MODEL = 1024
GRID_W = 64
EPS = 1e-6
ROPE_BASE = 10000.0

RET_HEADS = 4
RET_DK = 128
RET_DV = 256
RET_CHUNK = 128
RET_UNROLL = 4
RET_QK_W = RET_HEADS * RET_DK
RET_V_W = RET_HEADS * RET_DV

DIFF_HEADS = 8
DIFF_D = 64
DIFF_W = DIFF_HEADS * 2 * DIFF_D
DIFF_SUBTILES = 4

PEER_HEADS = 8
PEER_NKEYS = 128
PEER_EXPERTS = PEER_NKEYS * PEER_NKEYS
PEER_HALF = 128
PEER_TOPK = 16

COL_RQ = 0
COL_RK = COL_RQ + RET_QK_W
COL_RV = COL_RK + RET_QK_W
COL_RG = COL_RV + RET_V_W
COL_DQ = COL_RG + RET_V_W
COL_DK = COL_DQ + DIFF_W
COL_DV = COL_DK + DIFF_W
COL_GR = COL_DV + DIFF_W
COL_GD = COL_GR + D_MODEL
IN_COLS = COL_GD + D_MODEL

LANES = 128
BF16_ROWS = 16
VMEM_LIMIT = 56 << 20

NEG_INF = float("-inf")
LOG2_E = math.log2(math.e)

NT_DIMS = (((1,), (1,)), ((), ()))
TN_DIMS = (((0,), (0,)), ((), ()))


def _cparams(*sem, flags=None):
    return pltpu.CompilerParams(dimension_semantics=sem, vmem_limit_bytes=VMEM_LIMIT, flags=flags)


def _resident(shape):
    nd = len(shape)
    return pl.BlockSpec(shape, lambda *_: (0,) * nd, pipeline_mode=pl.Buffered(1))


def _mod_kernel(c_ref, w_ref, b_ref, o_ref):
    c = c_ref[...]
    s = c * jax.nn.sigmoid(c)
    o_ref[...] = jnp.dot(s, w_ref[...], preferred_element_type=F32,
                         precision=lax.Precision.HIGHEST) + b_ref[...]


def _modulation(cc, w, b):
    rows, d = cc.shape
    n = w.shape[1]
    tn = 768
    return pl.pallas_call(
        _mod_kernel,
        grid=(n // tn,),
        in_specs=[pl.BlockSpec((rows, d), lambda j: (0, 0)),
                  pl.BlockSpec((d, tn), lambda j: (0, j)),
                  pl.BlockSpec((1, tn), lambda j: (0, j))],
        out_specs=pl.BlockSpec((rows, tn), lambda j: (0, j)),
        out_shape=jax.ShapeDtypeStruct((rows, n), F32),
        compiler_params=_cparams("arbitrary"),
        name="modulation",
    )(cc, w, b.reshape(1, n))


def _rope_tables(seq, head_dim):
    rows = seq // GRID_W
    row = jnp.repeat(jnp.arange(rows, dtype=F32), GRID_W)
    col = jnp.tile(jnp.arange(GRID_W, dtype=F32), rows)
    half = head_dim // 2
    pair = half // 2
    lane = jnp.arange(LANES)
    d = lane % head_dim
    inv = ROPE_BASE ** (-jnp.arange(pair, dtype=F32) / pair)
    freq = inv[d % pair]
    pos = jnp.where((d < half)[None, :], row[:, None], col[:, None])
    ang = pos * freq[None, :]
    cos = jnp.cos(ang)
    sin = jnp.sin(ang)
    first = ((d % half) < pair)[None, :]
    sin_a = jnp.where(first, -sin, 0.0)
    sin_b = jnp.where(first, 0.0, sin)
    return cos, sin_a, sin_b, pair


def _rope(acc, cos, sin_a, sin_b, pair):
    up = pltpu.roll(acc, LANES - pair, 1)
    dn = pltpu.roll(acc, pair, 1)
    return acc * cos + up * sin_a + dn * sin_b


def _inproj_kernel(*refs, specs, rope, ret_pair, diff_pair):
    if rope:
        (x_ref, g_ref, sc_ref, sh_ref, w_ref,
         rc_ref, ra_ref, rb_ref, dc_ref, da_ref, db_ref) = refs[:11]
        out_refs = refs[11:]
    else:
        x_ref, g_ref, sc_ref, sh_ref, w_ref = refs[:5]
        out_refs = refs[5:]
    x = x_ref[...]
    y = x * lax.rsqrt(jnp.mean(x * x, axis=-1, keepdims=True) + EPS)
    u = (y * g_ref[...]) * (1.0 + sc_ref[0]) + sh_ref[0]
    ub = u.astype(BF16)
    for (col0, width, kind), o_ref in zip(specs, out_refs):
        for c in range(0, width, 512):
            cw = min(512, width - c)
            acc = jnp.dot(ub, w_ref[:, col0 + c:col0 + c + cw], preferred_element_type=F32)
            if kind in ("ret_q", "ret_k", "diff_q", "diff_k"):
                for l in range(0, cw, LANES):
                    a = acc[:, l:l + LANES]
                    if kind.startswith("ret"):
                        r = _rope(a, rc_ref[...], ra_ref[...], rb_ref[...], ret_pair)
                    else:
                        r = _rope(a, dc_ref[...], da_ref[...], db_ref[...], diff_pair)
                    if kind == "ret_q":
                        r = r * (RET_DK ** -0.5)
                    elif kind == "diff_q":
                        r = r * (DIFF_D ** -0.5 * LOG2_E)
                    o_ref[:, c + l:c + l + LANES] = r.astype(BF16)
            elif kind == "silu":
                o_ref[:, c:c + cw] = (acc * jax.nn.sigmoid(acc)).astype(BF16)
            elif kind == "sigmoid":
                o_ref[:, c:c + cw] = jax.nn.sigmoid(acc).astype(BF16)
            else:
                o_ref[:, c:c + cw] = acc.astype(BF16)


def _inproj(x2, gain, scale, shift, w_bf, specs, seq, tables):
    n, d = x2.shape
    tm = min(512, seq)
    assert seq % tm == 0 and n % seq == 0
    per_seq = seq // tm
    nb = scale.shape[0]
    if nb == 1:
        mod_map = lambda i: (0, 0, 0)
    else:
        mod_map = lambda i: (i // per_seq, 0, 0)
    rope = tables is not None
    in_specs = [pl.BlockSpec((tm, d), lambda i: (i, 0)),
                pl.BlockSpec((1, d), lambda i: (0, 0)),
                pl.BlockSpec((1, 1, d), mod_map),
                pl.BlockSpec((1, 1, d), mod_map),
                _resident(w_bf.shape)]
    args = [x2, gain.reshape(1, d), scale, shift, w_bf]
    ret_pair = diff_pair = 0
    if rope:
        (rc, ra, rb, ret_pair), (dc, da, db, diff_pair) = tables
        tab_spec = pl.BlockSpec((tm, LANES), lambda i: (i % per_seq, 0))
        in_specs += [tab_spec] * 6
        args += [rc, ra, rb, dc, da, db]
    out_specs = [pl.BlockSpec((tm, w), lambda i: (i, 0)) for (_, w, _) in specs]
    out_shape = [jax.ShapeDtypeStruct((n, w), BF16) for (_, w, _) in specs]
    kern = functools.partial(_inproj_kernel, specs=tuple(specs), rope=rope,
                             ret_pair=ret_pair, diff_pair=diff_pair)
    return pl.pallas_call(
        kern, grid=(n // tm,), in_specs=in_specs, out_specs=out_specs, out_shape=out_shape,
        compiler_params=_cparams("parallel"),
        name="inproj_rope" if rope else "inproj_ctx",
    )(*args)


def _ret_kernel(sdec_ref, q_ref, k_ref, v_ref, g_ref, kc_ref, vc_ref, m_ref, dec_ref, cdec_ref,
                gn_ref, o_ref, ybuf, kv, st, sf, sb, *, n_chunks):
    C = RET_CHUNK
    DK = RET_DK
    h = pl.program_id(1)
    sdec_f = sdec_ref[2 * h]
    sdec_b = sdec_ref[2 * h + 1]

    kc = kc_ref[...].astype(F32)
    vc = vc_ref[...]
    sf[...] = lax.dot_general((kc * cdec_ref[0, 0]).astype(BF16), vc, TN_DIMS,
                              preferred_element_type=F32)
    sb[...] = lax.dot_general((kc * cdec_ref[0, 1]).astype(BF16), vc, TN_DIMS,
                              preferred_element_type=F32)

    dmat = m_ref[0]
    qdec_f = dec_ref[0, 0]
    kdec_f = dec_ref[0, 1]
    qdec_b = dec_ref[0, 2]
    kdec_b = dec_ref[0, 3]
    gn = gn_ref[...]

    def intra(i, carry):
        r = pl.multiple_of(i * C, C)
        q = q_ref[pl.ds(r, C), :]
        k = k_ref[pl.ds(r, C), :]
        v = v_ref[pl.ds(r, C), :]
        kf = k.astype(F32)
        att = lax.dot_general(q, k, NT_DIMS, preferred_element_type=F32) * dmat
        ybuf[pl.ds(r, C), :] = jnp.dot(att.astype(BF16), v, preferred_element_type=F32)
        kk = jnp.concatenate([(kf * kdec_f).astype(BF16), (kf * kdec_b).astype(BF16)], axis=1)
        kv[i] = lax.dot_general(kk, v, TN_DIMS, preferred_element_type=F32)
        return carry

    lax.fori_loop(0, n_chunks, intra, 0, unroll=RET_UNROLL)

    def scan(t, carry):
        i = t
        st[i, :DK, :] = sf[...].astype(BF16)
        sf[...] = sdec_f * sf[...] + kv[i, :DK, :]
        j = n_chunks - 1 - t
        st[j, DK:, :] = sb[...].astype(BF16)
        sb[...] = sdec_b * sb[...] + kv[j, DK:, :]
        return carry

    lax.fori_loop(0, n_chunks, scan, 0)

    def cross(i, carry):
        r = pl.multiple_of(i * C, C)
        qf = q_ref[pl.ds(r, C), :].astype(F32)
        qq = jnp.concatenate([(qf * qdec_f).astype(BF16), (qf * qdec_b).astype(BF16)], axis=1)
        y = ybuf[pl.ds(r, C), :] + jnp.dot(qq, st[i], preferred_element_type=F32)
        mu = jnp.mean(y, axis=-1, keepdims=True)
        yc = y - mu
        var = jnp.mean(yc * yc, axis=-1, keepdims=True)
        yn = yc * lax.rsqrt(var + EPS) * gn
        o_ref[pl.ds(r, C), :] = (g_ref[pl.ds(r, C), :].astype(F32) * yn).astype(BF16)
        return carry

    lax.fori_loop(0, n_chunks, cross, 0, unroll=RET_UNROLL)


def _retention(ret_q, ret_k, ret_v, ret_g, ctx_rk, ctx_rv, dec_f, dec_b, gn_g, batch, seq, ctx_len):
    C = RET_CHUNK
    H = RET_HEADS
    lg_f = jnp.log1p(-jnp.exp2(dec_f.astype(F32)))
    lg_b = jnp.log1p(-jnp.exp2(dec_b.astype(F32)))
    idx = jnp.arange(C, dtype=F32)
    dist = idx[:, None] - idx[None, :]
    dmat = jnp.where(dist[None] >= 0,
                     jnp.exp(lg_f[:, None, None] * jnp.maximum(dist, 0.0)[None]),
                     jnp.exp(lg_b[:, None, None] * jnp.maximum(-dist, 0.0)[None]))
    qdec_f = jnp.exp(lg_f[:, None] * (idx + 1.0))
    kdec_f = jnp.exp(lg_f[:, None] * (C - 1.0 - idx))
    qdec_b = jnp.exp(lg_b[:, None] * (C - idx))
    kdec_b = jnp.exp(lg_b[:, None] * idx)
    dec = jnp.stack([qdec_f, kdec_f, qdec_b, kdec_b], axis=1)
    dec = jnp.broadcast_to(dec[..., None], (H, 4, C, RET_DK))
    cidx = jnp.arange(ctx_len, dtype=F32)
    cdec = jnp.stack([jnp.exp(lg_f[:, None] * (ctx_len - 1.0 - cidx)),
                      jnp.exp(lg_b[:, None] * cidx)], axis=1)
    cdec = jnp.broadcast_to(cdec[..., None], (H, 2, ctx_len, RET_DK))
    sdec = jnp.stack([jnp.exp(lg_f * C), jnp.exp(lg_b * C)], axis=1).reshape(2 * H)

    n_chunks = seq // C
    kern = functools.partial(_ret_kernel, n_chunks=n_chunks)
    return pl.pallas_call(
        kern,
        grid=(batch, H),
        in_specs=[pl.BlockSpec(memory_space=pltpu.SMEM),
                  pl.BlockSpec((seq, RET_DK), lambda b, h: (b, h)),
                  pl.BlockSpec((seq, RET_DK), lambda b, h: (b, h)),
                  pl.BlockSpec((seq, RET_DV), lambda b, h: (b, h)),
                  pl.BlockSpec((seq, RET_DV), lambda b, h: (b, h)),
                  pl.BlockSpec((ctx_len, RET_DK), lambda b, h: (b, h)),
                  pl.BlockSpec((ctx_len, RET_DV), lambda b, h: (b, h)),
                  pl.BlockSpec((1, C, C), lambda b, h: (h, 0, 0)),
                  pl.BlockSpec((1, 4, C, RET_DK), lambda b, h: (h, 0, 0, 0)),
                  pl.BlockSpec((1, 2, ctx_len, RET_DK), lambda b, h: (h, 0, 0, 0)),
                  pl.BlockSpec((1, RET_DV), lambda b, h: (0, h))],
        out_specs=pl.BlockSpec((seq, RET_DV), lambda b, h: (b, h)),
        out_shape=jax.ShapeDtypeStruct((batch * seq, RET_V_W), BF16),
        scratch_shapes=[pltpu.VMEM((seq, RET_DV), F32),
                        pltpu.VMEM((n_chunks, 2 * RET_DK, RET_DV), F32),
                        pltpu.VMEM((n_chunks, 2 * RET_DK, RET_DV), BF16),
                        pltpu.VMEM((RET_DK, RET_DV), F32),
                        pltpu.VMEM((RET_DK, RET_DV), F32)],
        compiler_params=_cparams("parallel", "arbitrary"),
        name="retention",
    )(sdec, ret_q, ret_k, ret_v, ret_g, ctx_rk, ctx_rv, dmat, dec, cdec,
      gn_g.reshape(1, RET_V_W))


def _diff_kernel(q_ref, kl_ref, kc_ref, vl_ref, vc_ref, lp_ref, gn_ref, o_ref, *, lam_init):
    kl = kl_ref[...]
    kc = kc_ref[...]
    tq, hw = q_ref.shape
    vl = jnp.concatenate([vl_ref[...], jnp.ones(vl_ref.shape, BF16)], axis=1)
    vc = jnp.concatenate([vc_ref[...], jnp.ones(vc_ref.shape, BF16)], axis=1)
    lp = lp_ref[...]
    lam = (jnp.exp(jnp.sum(lp[0:1] * lp[1:2], axis=-1, keepdims=True))
           - jnp.exp(jnp.sum(lp[2:3] * lp[3:4], axis=-1, keepdims=True)) + lam_init)

    def branch(qm):
        sl = lax.dot_general(qm, kl, NT_DIMS, preferred_element_type=F32)
        sc = lax.dot_general(qm, kc, NT_DIMS, preferred_element_type=F32)
        m = jnp.maximum(jnp.max(sl, axis=-1, keepdims=True), jnp.max(sc, axis=-1, keepdims=True))
        pl_ = jnp.exp2(sl - m)
        pc = jnp.exp2(sc - m)
        o = (jnp.dot(pl_.astype(BF16), vl, preferred_element_type=F32)
             + jnp.dot(pc.astype(BF16), vc, preferred_element_type=F32))
        return o[:, :hw] / o[:, hw:]

    sub = tq // DIFF_SUBTILES
    for t in range(DIFF_SUBTILES):
        rows = slice(t * sub, (t + 1) * sub)
        q = q_ref[rows, :]
        lane = lax.broadcasted_iota(jnp.int32, q.shape, 1)
        zero = jnp.zeros_like(q)
        o1 = branch(jnp.where(lane < DIFF_D, q, zero))
        o2 = branch(jnp.where(lane >= DIFF_D, q, zero))
        o = o1 - lam * o2
        on = o * lax.rsqrt(jnp.mean(o * o, axis=-1, keepdims=True) + EPS) * (1.0 - lam_init)
        o_ref[rows, :] = (on * gn_ref[...]).astype(BF16)


def _diff_attention(dq, dk, dv, ctx_dk, ctx_dv, lam_params, gn_g, lam_init, batch, seq, ctx_len):
    tq = min(512, seq)
    nq = seq // tq
    hw = 2 * DIFF_D
    kern = functools.partial(_diff_kernel, lam_init=lam_init)
    return pl.pallas_call(
        kern,
        grid=(batch, DIFF_HEADS, nq),
        in_specs=[pl.BlockSpec((tq, hw), lambda b, h, i: (b * nq + i, h)),
                  pl.BlockSpec((seq, hw), lambda b, h, i: (b, h)),
                  pl.BlockSpec((ctx_len, hw), lambda b, h, i: (b, h)),
                  pl.BlockSpec((seq, hw), lambda b, h, i: (b, h)),
                  pl.BlockSpec((ctx_len, hw), lambda b, h, i: (b, h)),
                  pl.BlockSpec((4, DIFF_D), lambda b, h, i: (0, 0)),
                  pl.BlockSpec((1, hw), lambda b, h, i: (0, h))],
        out_specs=pl.BlockSpec((tq, hw), lambda b, h, i: (b * nq + i, h)),
        out_shape=jax.ShapeDtypeStruct((batch * seq, DIFF_W), BF16),
        compiler_params=_cparams("parallel", "parallel", "arbitrary"),
        name="diff_attention",
    )(dq, dk, ctx_dk, dv, ctx_dv, lam_params, gn_g.reshape(1, DIFF_W))


def _merge_kernel(x_ref, zr_ref, zd_ref, gr_ref, gd_ref, wr_ref, wd_ref, wo_ref,
                  pmg_ref, pfg_ref, ga_ref, shf_ref, scf_ref, x1_ref, f_ref):
    p_ret = jnp.dot(zr_ref[...], wr_ref[...], preferred_element_type=F32)
    p_diff = jnp.dot(zd_ref[...], wd_ref[...], preferred_element_type=F32)
    m = gr_ref[...].astype(F32) * p_ret + gd_ref[...].astype(F32) * p_diff
    mix = jnp.dot(m.astype(BF16), wo_ref[...], preferred_element_type=F32)
    mixn = mix * lax.rsqrt(jnp.mean(mix * mix, axis=-1, keepdims=True) + EPS) * pmg_ref[...]
    x1 = x_ref[...] + ga_ref[0] * mixn
    x1_ref[...] = x1
    fn = x1 * lax.rsqrt(jnp.mean(x1 * x1, axis=-1, keepdims=True) + EPS) * pfg_ref[...]
    f_ref[...] = (fn * (1.0 + scf_ref[0]) + shf_ref[0]).astype(BF16)


def _merge(x2, z_ret, z_diff, g_r, g_d, w_br_ret, w_br_diff, w_out, post_mix_g, pre_ffn_g,
           ga_a, sh_f, sc_f, seq):
    n, d = x2.shape
    tm = min(512, seq)
    per_seq = seq // tm
    row = lambda i: (i, 0)
    mod = lambda i: (i // per_seq, 0, 0)
    vec = pl.BlockSpec((1, d), lambda i: (0, 0))
    return pl.pallas_call(
        _merge_kernel,
        grid=(n // tm,),
        in_specs=[pl.BlockSpec((tm, d), row)] * 5
                 + [_resident((d, d))] * 3
                 + [vec, vec]
                 + [pl.BlockSpec((1, 1, d), mod)] * 3,
        out_specs=[pl.BlockSpec((tm, d), row), pl.BlockSpec((tm, d), row)],
        out_shape=[jax.ShapeDtypeStruct((n, d), F32), jax.ShapeDtypeStruct((n, d), BF16)],
        compiler_params=_cparams("parallel"),
        name="merge",
    )(x2, z_ret, z_diff, g_r, g_d, w_br_ret.astype(BF16), w_br_diff.astype(BF16),
      w_out.astype(BF16), post_mix_g.reshape(1, d), pre_ffn_g.reshape(1, d), ga_a, sh_f, sc_f)


def _staircase():
    return [(p, q) for p in range(PEER_TOPK) for q in range(PEER_TOPK)
            if (p + 1) * (q + 1) <= PEER_TOPK]


def _bf16_pair_word(x):
    hi = pltpu.bitcast(x.astype(BF16).astype(F32), jnp.uint32)
    return hi | (hi >> 16)


def _bf16_rows(word_row):
    w = jnp.broadcast_to(word_row, (BF16_ROWS // 2, word_row.shape[1]))
    return pltpu.bitcast(w, BF16)


def _count_leading(rows, test, like):
    assert len(rows) == 16
    count = jnp.zeros_like(like)
    decisions = []
    for width in (8, 4, 2, 1):
        cands = [rows[base + width - 1] for base in range(0, 16, 2 * width)]
        for bit in reversed(decisions):
            cands = [jnp.where(bit, hi, lo) for lo, hi in zip(cands[0::2], cands[1::2])]
        passed = test(cands[0])
        decisions.append(passed)
        count = count + jnp.where(passed, float(width), 0.0)
    return jnp.where(test(rows[15]), 16.0, count)


def _sort16_pairs():
    pairs = []

    def merge(lo, hi, r):
        step = r * 2
        if step < hi - lo:
            merge(lo, hi, step)
            merge(lo + r, hi, step)
            pairs.extend((i, i + r) for i in range(lo + r, hi - r, step))
        else:
            pairs.append((lo, lo + r))

    def sort(lo, hi):
        if hi - lo >= 1:
            mid = lo + (hi - lo) // 2
            sort(lo, mid)
            sort(mid + 1, hi)
            merge(lo, hi, 1)

    sort(0, 15)
    return pairs


def _top16_sorted(slabs):
    cx = lambda a, b: (jnp.maximum(a, b), jnp.minimum(a, b))
    cur = list(slabs)
    for i, j in _sort16_pairs():
        cur[i], cur[j] = cx(cur[i], cur[j])
    for shift in (4, 2, 1):
        other = [pltpu.roll(x, shift, 0) for x in cur]
        cur = [jnp.maximum(cur[r], other[15 - r]) for r in range(16)]
        for stride in (8, 4, 2, 1):
            for i in range(16):
                if i & stride == 0:
                    cur[i], cur[i + stride] = cx(cur[i], cur[i + stride])
    return cur


def _route_kernel(fb_ref, wq_ref, sk_ref, r2_ref, n1_ref, e1_ref, e2_ref, s_scr, top):
    K = PEER_NKEYS
    G = 2 * PEER_HEADS
    H = PEER_HEADS
    tt = fb_ref.shape[0]
    qt = lax.dot_general(wq_ref[...], fb_ref[...], NT_DIMS, preferred_element_type=F32)
    for g in range(G):
        s_scr[g * K:(g + 1) * K, :] = jnp.dot(sk_ref[g], qt[g * K:(g + 1) * K, :].astype(BF16),
                                              preferred_element_type=F32)

    for g in range(G):
        hh, a = divmod(g, 2)
        for c in range(0, tt, LANES):
            lanes = slice(c, c + LANES)
            best = _top16_sorted([s_scr[g * K + r * 8:g * K + (r + 1) * 8, lanes]
                                  for r in range(PEER_TOPK)])
            for p in range(PEER_TOPK):
                row = (a * PEER_TOPK + p) * H + hh
                top[row:row + 1, lanes] = best[p][0:1, :]

    tops_a = [top[p * H:(p + 1) * H, :] for p in range(PEER_TOPK)]
    tops_b = [top[(PEER_TOPK + q) * H:(PEER_TOPK + q + 1) * H, :] for q in range(PEER_TOPK)]
    pairs = _staircase()
    cand = [tops_a[p] + tops_b[q] for (p, q) in pairs]
    cur = list(cand)
    tau = None
    for it in range(PEER_TOPK):
        tau = functools.reduce(jnp.maximum, cur)
        if it + 1 < PEER_TOPK:
            cur = [jnp.where(c == tau, NEG_INF, c) for c in cur]
    c00 = cand[0]
    z = functools.reduce(
        lambda a, b: a + b,
        [jnp.where(c >= tau, jnp.exp(c - c00), 0.0) for c in cand])
    zinv = 1.0 / z

    for hh in range(H):
        s1 = s_scr[(2 * hh) * K:(2 * hh + 1) * K, :]
        s2 = s_scr[(2 * hh + 1) * K:(2 * hh + 2) * K, :]
        tau_h = tau[hh:hh + 1, :]
        desc = [tops_b[q][hh:hh + 1, :] for q in range(PEER_TOPK)]
        cnt = _count_leading(desc, lambda b: s1 + b >= tau_h, s1)
        n1_ref[hh] = _bf16_pair_word(cnt)
        e1_ref[hh] = _bf16_pair_word(jnp.exp(s1 - tops_a[0][hh:hh + 1, :]) * zinv[hh:hh + 1, :])
        e2 = jnp.exp(s2 - tops_b[0][hh:hh + 1, :]).astype(BF16)
        e2_ref[hh] = pltpu.bitcast(e2, jnp.uint32)
        asc = [tops_b[PEER_TOPK - 1 - q][hh:hh + 1, :] for q in range(PEER_TOPK)]
        rank = float(PEER_TOPK) - _count_leading(asc, lambda b: s2 >= b, s2)
        r2_ref[hh] = pltpu.bitcast(rank.astype(BF16), jnp.uint32)


def _route(f, w_q, sub_keys, seq):
    n, d = f.shape
    tt = min(256, seq)
    H, K = PEER_HEADS, PEER_NKEYS
    wq_t = w_q.T.astype(BF16)
    sk = sub_keys.reshape(2 * H, K, PEER_HALF).astype(BF16)
    row_tab = jax.ShapeDtypeStruct((H, K, n), jnp.uint32)
    key_tab = jax.ShapeDtypeStruct((H, K // 2, n), jnp.uint32)
    row_spec = pl.BlockSpec((H, K, tt), lambda t: (0, 0, t))
    key_spec = pl.BlockSpec((H, K // 2, tt), lambda t: (0, 0, t))
    return pl.pallas_call(
        _route_kernel,
        grid=(n // tt,),
        in_specs=[pl.BlockSpec((tt, d), lambda t: (t, 0)),
                  _resident(wq_t.shape), _resident(sk.shape)],
        out_specs=[key_spec, row_spec, row_spec, key_spec],
        out_shape=[key_tab, row_tab, row_tab, key_tab],
        scratch_shapes=[pltpu.VMEM((2 * H * K, tt), F32),
                        pltpu.VMEM((2 * PEER_TOPK * H, tt), F32)],
        compiler_params=_cparams("parallel"),
        name="peer_route",
    )(f, wq_t, sk)


def _gelu(x):
    return 0.5 * x * (1.0 + lax.erf(x * (2.0 ** -0.5)))


def _expert_kernel(fb_ref, u_ref, vt_ref, r2_ref, e2_ref, n1_ref, e1_ref, x1_ref, g_ref, ga_ref,
                   o_ref, acc, h_scr, w_scr, *, rows_per_tile):
    e = pl.program_id(1)
    K = PEER_NKEYS

    @pl.when(e == 0)
    def _():
        acc[...] = jnp.zeros_like(acc)

    tt = fb_ref.shape[0]
    groups = K // BF16_ROWS
    zero = jnp.zeros((BF16_ROWS, tt), BF16)
    h_scr[...] = lax.dot_general(u_ref[...], fb_ref[...], NT_DIMS, preferred_element_type=F32)
    for i in range(rows_per_tile):
        gate = [None] * groups
        for hh in range(PEER_HEADS):
            n1 = _bf16_rows(n1_ref[hh, i:i + 1, :])
            e1 = _bf16_rows(e1_ref[hh, i:i + 1, :])
            for r in range(groups):
                words = slice(r * BF16_ROWS // 2, (r + 1) * BF16_ROWS // 2)
                r2 = pltpu.bitcast(r2_ref[hh, words, :], BF16)
                e2 = pltpu.bitcast(e2_ref[hh, words, :], BF16)
                term = jnp.where(r2 < n1, e1 * e2, zero)
                gate[r] = term if hh == 0 else gate[r] + term
        for r in range(groups):
            rows = slice(i * K + r * BF16_ROWS, i * K + (r + 1) * BF16_ROWS)
            w_scr[rows, :] = gate[r] * _gelu(h_scr[rows, :].astype(BF16))
    acc[...] += jnp.dot(vt_ref[...], w_scr[...], preferred_element_type=F32)

    @pl.when(e == pl.num_programs(1) - 1)
    def _():
        y = acc[...]
        yn = y * lax.rsqrt(jnp.mean(y * y, axis=0, keepdims=True) + EPS)
        o_ref[...] = x1_ref[...] + ga_ref[0] * (yn.T * g_ref[...])


def _experts(fb, u_bf, vt_bf, r2, n1, e1, e2, x1, post_ffn_g, ga_f, seq):
    n, d = fb.shape
    tt = min(512, seq)
    per_seq = seq // tt
    rows_per_tile = 8
    et = rows_per_tile * PEER_NKEYS
    H, K = PEER_HEADS, PEER_NKEYS
    kern = functools.partial(_expert_kernel, rows_per_tile=rows_per_tile)
    return pl.pallas_call(
        kern,
        grid=(n // tt, PEER_EXPERTS // et),
        in_specs=[pl.BlockSpec((tt, d), lambda t, e: (t, 0)),
                  pl.BlockSpec((et, d), lambda t, e: (e, 0)),
                  pl.BlockSpec((d, et), lambda t, e: (0, e)),
                  pl.BlockSpec((H, K // 2, tt), lambda t, e: (0, 0, t)),
                  pl.BlockSpec((H, K // 2, tt), lambda t, e: (0, 0, t)),
                  pl.BlockSpec((H, rows_per_tile, tt), lambda t, e: (0, e, t)),
                  pl.BlockSpec((H, rows_per_tile, tt), lambda t, e: (0, e, t)),
                  pl.BlockSpec((tt, d), lambda t, e: (t, 0)),
                  pl.BlockSpec((1, d), lambda t, e: (0, 0)),
                  pl.BlockSpec((1, 1, d), lambda t, e: (t // per_seq, 0, 0))],
        out_specs=pl.BlockSpec((tt, d), lambda t, e: (t, 0)),
        out_shape=jax.ShapeDtypeStruct((n, d), F32),
        scratch_shapes=[pltpu.VMEM((d, tt), F32), pltpu.VMEM((et, tt), F32),
                        pltpu.VMEM((et, tt), BF16)],
        compiler_params=_cparams("parallel", "arbitrary"),
        name="peer_experts",
    )(fb, u_bf, vt_bf, r2, e2, n1, e1, x1, post_ffn_g.reshape(1, d), ga_f)


def kernel(x, c, ctx, c_ctx, w_mod, b_mod, pre_mix_g, post_mix_g, pre_ffn_g, post_ffn_g, w_in,
           ret_decay_fwd, ret_decay_bwd, ret_gn_g, diff_lambda, diff_gn_g, w_br_ret, w_br_diff,
           w_out, peer_w_q, peer_sub_keys, peer_u, peer_v):
    batch, seq, d = x.shape
    ctx_len = ctx.shape[1]
    depth = w_mod.shape[0]
    assert depth == 1 and d == D_MODEL
    l = 0
    lam_init = 0.8 - 0.6 * math.exp(-0.3 * l)

    rows = ((batch + 1 + 7) // 8) * 8
    cc = jnp.zeros((rows, d), F32).at[:batch].set(c).at[batch].set(c_ctx)
    mod = _modulation(cc, w_mod[l], b_mod[l])
    sh_a, sc_a, ga_a, sh_f, sc_f, ga_f = [t[:batch, None, :] for t in jnp.split(mod, 6, axis=-1)]
    csh_a, csc_a = [t[batch:batch + 1, None, :] for t in jnp.split(mod, 6, axis=-1)[:2]]

    w_in_bf = w_in[l].astype(BF16)
    x2 = x.reshape(batch * seq, d)
    ctx2 = ctx.reshape(batch * ctx_len, d)

    lat_specs = [(COL_RQ, RET_QK_W, "ret_q"), (COL_RK, RET_QK_W, "ret_k"),
                 (COL_RV, RET_V_W, "plain"), (COL_RG, RET_V_W, "silu"),
                 (COL_DQ, DIFF_W, "diff_q"), (COL_DK, DIFF_W, "diff_k"),
                 (COL_DV, DIFF_W, "plain"), (COL_GR, D_MODEL, "sigmoid"),
                 (COL_GD, D_MODEL, "sigmoid")]
    tables = (_rope_tables(seq, RET_DK), _rope_tables(seq, DIFF_D))
    rq, rk, rv, rg, dq, dk, dv, g_r, g_d = _inproj(
        x2, pre_mix_g[l], sc_a, sh_a, w_in_bf, lat_specs, seq, tables)

    ctx_specs = [(COL_RK, RET_QK_W, "plain"), (COL_RV, RET_V_W, "plain"),
                 (COL_DK, DIFF_W, "plain"), (COL_DV, DIFF_W, "plain")]
    rk_c, rv_c, dk_c, dv_c = _inproj(ctx2, pre_mix_g[l], csc_a, csh_a, w_in_bf, ctx_specs,
                                     ctx_len, None)

    z_ret = _retention(rq, rk, rv, rg, rk_c, rv_c, ret_decay_fwd[l], ret_decay_bwd[l],
                       ret_gn_g[l], batch, seq, ctx_len)
    z_diff = _diff_attention(dq, dk, dv, dk_c, dv_c, diff_lambda[l], diff_gn_g[l], lam_init,
                             batch, seq, ctx_len)
    x1, fb = _merge(x2, z_ret, z_diff, g_r, g_d, w_br_ret[l], w_br_diff[l], w_out[l],
                   post_mix_g[l], pre_ffn_g[l], ga_a, sh_f, sc_f, seq)

    r2, n1, e1, e2 = _route(fb, peer_w_q[l], peer_sub_keys[l], seq)
    out = _experts(fb, peer_u[l].astype(BF16), peer_v[l].T.astype(BF16), r2, n1, e1, e2, x1,
                   post_ffn_g[l], ga_f, seq)
    return out.reshape(batch, seq, d)
```

```python
import functools
import math

import jax
import jax.numpy as jnp
from jax import lax
from jax.experimental import pallas as pl
from jax.experimental.pallas import tpu as pltpu

F32 = jnp.float32
BF16 = jnp.bfloat16

D_MODEL = 1024
GRID_W = 64
EPS = 1e-6
ROPE_BASE = 10000.0

RET_HEADS = 4
RET_DK = 128
RET_DV = 256
RET_CHUNK = 128
RET_UNROLL = 4
RET_QK_W = RET_HEADS * RET_DK
RET_V_W = RET_HEADS * RET_DV

DIFF_HEADS = 8
DIFF_D = 64
DIFF_W = DIFF_HEADS * 2 * DIFF_D
MERGE_SUBTILES = 2
DIFF_Q_TILE = 2048
DIFF_SUB_ROWS = 128

PEER_HEADS = 8
PEER_NKEYS = 128
PEER_EXPERTS = PEER_NKEYS * PEER_NKEYS
PEER_HALF = 128
PEER_TOPK = 16

COL_RQ = 0
COL_RK = COL_RQ + RET_QK_W
COL_RV = COL_RK + RET_QK_W
COL_RG = COL_RV + RET_V_W
COL_DQ = COL_RG + RET_V_W
COL_DK = COL_DQ + DIFF_W
COL_DV = COL_DK + DIFF_W
COL_GR = COL_DV + DIFF_W
COL_GD = COL_GR + D_MODEL
IN_COLS = COL_GD + D_MODEL

LANES = 128
BF16_ROWS = 16
EXPERT_TILE_ROWS = 16
EXPERT_ROW_CHUNKS = 8
VMEM_LIMIT = 56 << 20

NEG_INF = float("-inf")
LOG2_E = math.log2(math.e)

NT_DIMS = (((1,), (1,)), ((), ()))
TN_DIMS = (((0,), (0,)), ((), ()))


def _cparams(*sem, flags=None):
    return pltpu.CompilerParams(dimension_semantics=sem, vmem_limit_bytes=VMEM_LIMIT, flags=flags)


def _resident(shape):
    nd = len(shape)
    return pl.BlockSpec(shape, lambda *_: (0,) * nd, pipeline_mode=pl.Buffered(1))


def _mod_kernel(c_ref, w_ref, b_ref, o_ref):
    c = c_ref[...]
    s = c * jax.nn.sigmoid(c)
    o_ref[...] = jnp.dot(s, w_ref[...], preferred_element_type=F32,
                         precision=lax.Precision.HIGHEST) + b_ref[...]


def _modulation(cc, w, b):
    rows, d = cc.shape
    n = w.shape[1]
    tn = 768
    return pl.pallas_call(
        _mod_kernel,
        grid=(n // tn,),
        in_specs=[pl.BlockSpec((rows, d), lambda j: (0, 0)),
                  pl.BlockSpec((d, tn), lambda j: (0, j)),
                  pl.BlockSpec((1, tn), lambda j: (0, j))],
        out_specs=pl.BlockSpec((rows, tn), lambda j: (0, j)),
        out_shape=jax.ShapeDtypeStruct((rows, n), F32),
        compiler_params=_cparams("arbitrary"),
        name="modulation",
    )(cc, w, b.reshape(1, n))


def _rope_tables(seq, head_dim):
    rows = seq // GRID_W
    row = jnp.repeat(jnp.arange(rows, dtype=F32), GRID_W)
    col = jnp.tile(jnp.arange(GRID_W, dtype=F32), rows)
    half = head_dim // 2
    pair = half // 2
    lane = jnp.arange(LANES)
    d = lane % head_dim
    inv = ROPE_BASE ** (-jnp.arange(pair, dtype=F32) / pair)
    freq = inv[d % pair]
    pos = jnp.where((d < half)[None, :], row[:, None], col[:, None])
    ang = pos * freq[None, :]
    cos = jnp.cos(ang)
    sin = jnp.sin(ang)
    first = ((d % half) < pair)[None, :]
    sin_a = jnp.where(first, -sin, 0.0)
    sin_b = jnp.where(first, 0.0, sin)
    return cos, sin_a, sin_b, pair


def _rope(acc, cos, sin_a, sin_b, pair):
    up = pltpu.roll(acc, LANES - pair, 1)
    dn = pltpu.roll(acc, pair, 1)
    return acc * cos + up * sin_a + dn * sin_b


def _inproj_kernel(*refs, specs, rope, ret_pair, diff_pair):
    if rope:
        (x_ref, g_ref, sc_ref, sh_ref, w_ref,
         rc_ref, ra_ref, rb_ref, dc_ref, da_ref, db_ref) = refs[:11]
        out_refs = refs[11:]
    else:
        x_ref, g_ref, sc_ref, sh_ref, w_ref = refs[:5]
        out_refs = refs[5:]
    x = x_ref[...]
    y = x * lax.rsqrt(jnp.mean(x * x, axis=-1, keepdims=True) + EPS)
    u = (y * g_ref[...]) * (1.0 + sc_ref[0]) + sh_ref[0]
    ub = u.astype(BF16)
    for (col0, width, kind), o_ref in zip(specs, out_refs):
        for c in range(0, width, 512):
            cw = min(512, width - c)
            acc = jnp.dot(ub, w_ref[:, col0 + c:col0 + c + cw], preferred_element_type=F32)
            if kind in ("ret_q", "ret_k", "diff_q", "diff_k"):
                for l in range(0, cw, LANES):
                    a = acc[:, l:l + LANES]
                    if kind.startswith("ret"):
                        r = _rope(a, rc_ref[...], ra_ref[...], rb_ref[...], ret_pair)
                    else:
                        r = _rope(a, dc_ref[...], da_ref[...], db_ref[...], diff_pair)
                    if kind == "ret_q":
                        r = r * (RET_DK ** -0.5)
                    elif kind == "diff_q":
                        r = r * (DIFF_D ** -0.5 * LOG2_E)
                    o_ref[:, c + l:c + l + LANES] = r.astype(BF16)
            elif kind == "silu":
                o_ref[:, c:c + cw] = (acc * jax.nn.sigmoid(acc)).astype(BF16)
            elif kind == "sigmoid":
                o_ref[:, c:c + cw] = jax.nn.sigmoid(acc).astype(BF16)
            else:
                o_ref[:, c:c + cw] = acc.astype(BF16)


def _inproj(x2, gain, scale, shift, w_bf, specs, seq, tables):
    n, d = x2.shape
    tm = min(512, seq)
    assert seq % tm == 0 and n % seq == 0
    per_seq = seq // tm
    nb = scale.shape[0]
    if nb == 1:
        mod_map = lambda i: (0, 0, 0)
    else:
        mod_map = lambda i: (i // per_seq, 0, 0)
    rope = tables is not None
    in_specs = [pl.BlockSpec((tm, d), lambda i: (i, 0)),
                pl.BlockSpec((1, d), lambda i: (0, 0)),
                pl.BlockSpec((1, 1, d), mod_map),
                pl.BlockSpec((1, 1, d), mod_map),
                _resident(w_bf.shape)]
    args = [x2, gain.reshape(1, d), scale, shift, w_bf]
    ret_pair = diff_pair = 0
    if rope:
        (rc, ra, rb, ret_pair), (dc, da, db, diff_pair) = tables
        tab_spec = pl.BlockSpec((tm, LANES), lambda i: (i % per_seq, 0))
        in_specs += [tab_spec] * 6
        args += [rc, ra, rb, dc, da, db]
    out_specs = [pl.BlockSpec((tm, w), lambda i: (i, 0)) for (_, w, _) in specs]
    out_shape = [jax.ShapeDtypeStruct((n, w), BF16) for (_, w, _) in specs]
    kern = functools.partial(_inproj_kernel, specs=tuple(specs), rope=rope,
                             ret_pair=ret_pair, diff_pair=diff_pair)
    return pl.pallas_call(
        kern, grid=(n // tm,), in_specs=in_specs, out_specs=out_specs, out_shape=out_shape,
        compiler_params=_cparams("parallel"),
        name="inproj_rope" if rope else "inproj_ctx",
    )(*args)


def _ret_kernel(sdec_ref, q_ref, k_ref, v_ref, g_ref, kc_ref, vc_ref, m_ref, dec_ref, cdec_ref,
                gn_ref, o_ref, ybuf, kv, st, sf, sb, *, n_chunks):
    C = RET_CHUNK
    DK = RET_DK
    h = pl.program_id(1)
    sdec_f = sdec_ref[2 * h]
    sdec_b = sdec_ref[2 * h + 1]

    kc = kc_ref[...].astype(F32)
    vc = vc_ref[...]
    sf[...] = lax.dot_general((kc * cdec_ref[0, 0]).astype(BF16), vc, TN_DIMS,
                              preferred_element_type=F32)
    sb[...] = lax.dot_general((kc * cdec_ref[0, 1]).astype(BF16), vc, TN_DIMS,
                              preferred_element_type=F32)

    dmat = m_ref[0]
    qdec_f = dec_ref[0, 0]
    kdec_f = dec_ref[0, 1]
    qdec_b = dec_ref[0, 2]
    kdec_b = dec_ref[0, 3]
    gn = gn_ref[...]

    def intra(i, carry):
        r = pl.multiple_of(i * C, C)
        q = q_ref[pl.ds(r, C), :]
        k = k_ref[pl.ds(r, C), :]
        v = v_ref[pl.ds(r, C), :]
        kf = k.astype(F32)
        att = lax.dot_general(q, k, NT_DIMS, preferred_element_type=F32) * dmat
        ybuf[pl.ds(r, C), :] = jnp.dot(att.astype(BF16), v, preferred_element_type=F32)
        kk = jnp.concatenate([(kf * kdec_f).astype(BF16), (kf * kdec_b).astype(BF16)], axis=1)
        kv[i] = lax.dot_general(kk, v, TN_DIMS, preferred_element_type=F32)
        return carry

    lax.fori_loop(0, n_chunks, intra, 0, unroll=RET_UNROLL)

    def scan(t, carry):
        i = t
        st[i, :DK, :] = sf[...].astype(BF16)
        sf[...] = sdec_f * sf[...] + kv[i, :DK, :]
        j = n_chunks - 1 - t
        st[j, DK:, :] = sb[...].astype(BF16)
        sb[...] = sdec_b * sb[...] + kv[j, DK:, :]
        return carry

    lax.fori_loop(0, n_chunks, scan, 0)

    def cross(i, carry):
        r = pl.multiple_of(i * C, C)
        qf = q_ref[pl.ds(r, C), :].astype(F32)
        qq = jnp.concatenate([(qf * qdec_f).astype(BF16), (qf * qdec_b).astype(BF16)], axis=1)
        y = ybuf[pl.ds(r, C), :] + jnp.dot(qq, st[i], preferred_element_type=F32)
        mu = jnp.mean(y, axis=-1, keepdims=True)
        yc = y - mu
        var = jnp.mean(yc * yc, axis=-1, keepdims=True)
        yn = yc * lax.rsqrt(var + EPS) * gn
        o_ref[pl.ds(r, C), :] = (g_ref[pl.ds(r, C), :].astype(F32) * yn).astype(BF16)
        return carry

    lax.fori_loop(0, n_chunks, cross, 0, unroll=RET_UNROLL)


def _retention(ret_q, ret_k, ret_v, ret_g, ctx_rk, ctx_rv, dec_f, dec_b, gn_g, batch, seq, ctx_len):
    C = RET_CHUNK
    H = RET_HEADS
    lg_f = jnp.log1p(-jnp.exp2(dec_f.astype(F32)))
    lg_b = jnp.log1p(-jnp.exp2(dec_b.astype(F32)))
    idx = jnp.arange(C, dtype=F32)
    dist = idx[:, None] - idx[None, :]
    dmat = jnp.where(dist[None] >= 0,
                     jnp.exp(lg_f[:, None, None] * jnp.maximum(dist, 0.0)[None]),
                     jnp.exp(lg_b[:, None, None] * jnp.maximum(-dist, 0.0)[None]))
    qdec_f = jnp.exp(lg_f[:, None] * (idx + 1.0))
    kdec_f = jnp.exp(lg_f[:, None] * (C - 1.0 - idx))
    qdec_b = jnp.exp(lg_b[:, None] * (C - idx))
    kdec_b = jnp.exp(lg_b[:, None] * idx)
    dec = jnp.stack([qdec_f, kdec_f, qdec_b, kdec_b], axis=1)
    dec = jnp.broadcast_to(dec[..., None], (H, 4, C, RET_DK))
    cidx = jnp.arange(ctx_len, dtype=F32)
    cdec = jnp.stack([jnp.exp(lg_f[:, None] * (ctx_len - 1.0 - cidx)),
                      jnp.exp(lg_b[:, None] * cidx)], axis=1)
    cdec = jnp.broadcast_to(cdec[..., None], (H, 2, ctx_len, RET_DK))
    sdec = jnp.stack([jnp.exp(lg_f * C), jnp.exp(lg_b * C)], axis=1).reshape(2 * H)

    n_chunks = seq // C
    kern = functools.partial(_ret_kernel, n_chunks=n_chunks)
    return pl.pallas_call(
        kern,
        grid=(batch, H),
        in_specs=[pl.BlockSpec(memory_space=pltpu.SMEM),
                  pl.BlockSpec((seq, RET_DK), lambda b, h: (b, h)),
                  pl.BlockSpec((seq, RET_DK), lambda b, h: (b, h)),
                  pl.BlockSpec((seq, RET_DV), lambda b, h: (b, h)),
                  pl.BlockSpec((seq, RET_DV), lambda b, h: (b, h)),
                  pl.BlockSpec((ctx_len, RET_DK), lambda b, h: (b, h)),
                  pl.BlockSpec((ctx_len, RET_DV), lambda b, h: (b, h)),
                  pl.BlockSpec((1, C, C), lambda b, h: (h, 0, 0)),
                  pl.BlockSpec((1, 4, C, RET_DK), lambda b, h: (h, 0, 0, 0)),
                  pl.BlockSpec((1, 2, ctx_len, RET_DK), lambda b, h: (h, 0, 0, 0)),
                  pl.BlockSpec((1, RET_DV), lambda b, h: (0, h))],
        out_specs=pl.BlockSpec((seq, RET_DV), lambda b, h: (b, h)),
        out_shape=jax.ShapeDtypeStruct((batch * seq, RET_V_W), BF16),
        scratch_shapes=[pltpu.VMEM((seq, RET_DV), F32),
                        pltpu.VMEM((n_chunks, 2 * RET_DK, RET_DV), F32),
                        pltpu.VMEM((n_chunks, 2 * RET_DK, RET_DV), BF16),
                        pltpu.VMEM((RET_DK, RET_DV), F32),
                        pltpu.VMEM((RET_DK, RET_DV), F32)],
        compiler_params=_cparams("parallel", "arbitrary"),
        name="retention",
    )(sdec, ret_q, ret_k, ret_v, ret_g, ctx_rk, ctx_rv, dmat, dec, cdec,
      gn_g.reshape(1, RET_V_W))


def _diff_kernel(q_ref, kl_ref, kc_ref, vl_ref, vc_ref, lp_ref, gn_ref, o_ref, *, lam_init):
    kl = kl_ref[...]
    kc = kc_ref[...]
    tq, hw = q_ref.shape
    vl = jnp.concatenate([vl_ref[...], jnp.ones(vl_ref.shape, BF16)], axis=1)
    vc = jnp.concatenate([vc_ref[...], jnp.ones(vc_ref.shape, BF16)], axis=1)
    lp = lp_ref[...]
    lam = (jnp.exp(jnp.sum(lp[0:1] * lp[1:2], axis=-1, keepdims=True))
           - jnp.exp(jnp.sum(lp[2:3] * lp[3:4], axis=-1, keepdims=True)) + lam_init)

    def branch(qm):
        sl = lax.dot_general(qm, kl, NT_DIMS, preferred_element_type=F32)
        sc = lax.dot_general(qm, kc, NT_DIMS, preferred_element_type=F32)
        m = jnp.maximum(jnp.max(sl, axis=-1, keepdims=True), jnp.max(sc, axis=-1, keepdims=True))
        pl_ = jnp.exp2(sl - m)
        pc = jnp.exp2(sc - m)
        o = (jnp.dot(pl_.astype(BF16), vl, preferred_element_type=F32)
             + jnp.dot(pc.astype(BF16), vc, preferred_element_type=F32))
        return o[:, :hw] / o[:, hw:]

    sub = DIFF_SUB_ROWS
    for t in range(tq // sub):
        rows = slice(t * sub, (t + 1) * sub)
        q = q_ref[rows, :]
        lane = lax.broadcasted_iota(jnp.int32, q.shape, 1)
        zero = jnp.zeros_like(q)
        o1 = branch(jnp.where(lane < DIFF_D, q, zero))
        o2 = branch(jnp.where(lane >= DIFF_D, q, zero))
        o = o1 - lam * o2
        on = o * lax.rsqrt(jnp.mean(o * o, axis=-1, keepdims=True) + EPS) * (1.0 - lam_init)
        o_ref[rows, :] = (on * gn_ref[...]).astype(BF16)


def _diff_attention(dq, dk, dv, ctx_dk, ctx_dv, lam_params, gn_g, lam_init, batch, seq, ctx_len):
    tq = min(DIFF_Q_TILE, seq)
    nq = seq // tq
    hw = 2 * DIFF_D
    kern = functools.partial(_diff_kernel, lam_init=lam_init)
    return pl.pallas_call(
        kern,
        grid=(batch, DIFF_HEADS, nq),
        in_specs=[pl.BlockSpec((tq, hw), lambda b, h, i: (b * nq + i, h)),
                  pl.BlockSpec((seq, hw), lambda b, h, i: (b, h)),
                  pl.BlockSpec((ctx_len, hw), lambda b, h, i: (b, h)),
                  pl.BlockSpec((seq, hw), lambda b, h, i: (b, h)),
                  pl.BlockSpec((ctx_len, hw), lambda b, h, i: (b, h)),
                  pl.BlockSpec((4, DIFF_D), lambda b, h, i: (0, 0)),
                  pl.BlockSpec((1, hw), lambda b, h, i: (0, h))],
        out_specs=pl.BlockSpec((tq, hw), lambda b, h, i: (b * nq + i, h)),
        out_shape=jax.ShapeDtypeStruct((batch * seq, DIFF_W), BF16),
        compiler_params=_cparams("parallel", "parallel", "arbitrary"),
        name="diff_attention",
    )(dq, dk, ctx_dk, dv, ctx_dv, lam_params, gn_g.reshape(1, DIFF_W))


def _merge_kernel(x_ref, zr_ref, zd_ref, gr_ref, gd_ref, wr_ref, wd_ref, wo_ref,
                  pmg_ref, pfg_ref, ga_ref, shf_ref, scf_ref, x1_ref, f_ref):
    sub = x_ref.shape[0] // MERGE_SUBTILES
    for t in range(MERGE_SUBTILES):
        rows = slice(t * sub, (t + 1) * sub)
        p_ret = jnp.dot(zr_ref[rows, :], wr_ref[...], preferred_element_type=F32)
        p_diff = jnp.dot(zd_ref[rows, :], wd_ref[...], preferred_element_type=F32)
        m = gr_ref[rows, :].astype(F32) * p_ret + gd_ref[rows, :].astype(F32) * p_diff
        mix = jnp.dot(m.astype(BF16), wo_ref[...], preferred_element_type=F32)
        mixn = mix * lax.rsqrt(jnp.mean(mix * mix, axis=-1, keepdims=True) + EPS) * pmg_ref[...]
        x1 = x_ref[rows, :] + ga_ref[0] * mixn
        x1_ref[rows, :] = x1
        fn = x1 * lax.rsqrt(jnp.mean(x1 * x1, axis=-1, keepdims=True) + EPS) * pfg_ref[...]
        f_ref[rows, :] = (fn * (1.0 + scf_ref[0]) + shf_ref[0]).astype(BF16)


def _merge(x2, z_ret, z_diff, g_r, g_d, w_br_ret, w_br_diff, w_out, post_mix_g, pre_ffn_g,
           ga_a, sh_f, sc_f, seq):
    n, d = x2.shape
    tm = min(512, seq)
    per_seq = seq // tm
    row = lambda i: (i, 0)
    mod = lambda i: (i // per_seq, 0, 0)
    vec = pl.BlockSpec((1, d), lambda i: (0, 0))
    return pl.pallas_call(
        _merge_kernel,
        grid=(n // tm,),
        in_specs=[pl.BlockSpec((tm, d), row)] * 5
                 + [_resident((d, d))] * 3
                 + [vec, vec]
                 + [pl.BlockSpec((1, 1, d), mod)] * 3,
        out_specs=[pl.BlockSpec((tm, d), row), pl.BlockSpec((tm, d), row)],
        out_shape=[jax.ShapeDtypeStruct((n, d), F32), jax.ShapeDtypeStruct((n, d), BF16)],
        compiler_params=_cparams("parallel"),
        name="merge",
    )(x2, z_ret, z_diff, g_r, g_d, w_br_ret.astype(BF16), w_br_diff.astype(BF16),
      w_out.astype(BF16), post_mix_g.reshape(1, d), pre_ffn_g.reshape(1, d), ga_a, sh_f, sc_f)


def _staircase():
    return [(p, q) for p in range(PEER_TOPK) for q in range(PEER_TOPK)
            if (p + 1) * (q + 1) <= PEER_TOPK]


def _bf16_pair_word(x):
    hi = pltpu.bitcast(x.astype(BF16).astype(F32), jnp.uint32)
    return hi | (hi >> 16)


def _bf16_rows(word_row):
    w = jnp.broadcast_to(word_row, (BF16_ROWS // 2, word_row.shape[1]))
    return pltpu.bitcast(w, BF16)


def _count_leading(rows, test, like):
    assert len(rows) == 16
    count = jnp.zeros_like(like)
    decisions = []
    for width in (8, 4, 2, 1):
        cands = [rows[base + width - 1] for base in range(0, 16, 2 * width)]
        for bit in reversed(decisions):
            cands = [jnp.where(bit, hi, lo) for lo, hi in zip(cands[0::2], cands[1::2])]
        passed = test(cands[0])
        decisions.append(passed)
        count = count + jnp.where(passed, float(width), 0.0)
    return jnp.where(test(rows[15]), 16.0, count)


def _sort16_pairs():
    pairs = []

    def merge(lo, hi, r):
        step = r * 2
        if step < hi - lo:
            merge(lo, hi, step)
            merge(lo + r, hi, step)
            pairs.extend((i, i + r) for i in range(lo + r, hi - r, step))
        else:
            pairs.append((lo, lo + r))

    def sort(lo, hi):
        if hi - lo >= 1:
            mid = lo + (hi - lo) // 2
            sort(lo, mid)
            sort(mid + 1, hi)
            merge(lo, hi, 1)

    sort(0, 15)
    return pairs


def _top16_sorted(slabs):
    cx = lambda a, b: (jnp.maximum(a, b), jnp.minimum(a, b))
    cur = list(slabs)
    for i, j in _sort16_pairs():
        cur[i], cur[j] = cx(cur[i], cur[j])
    for shift in (4, 2, 1):
        other = [pltpu.roll(x, shift, 0) for x in cur]
        cur = [jnp.maximum(cur[r], other[15 - r]) for r in range(16)]
        for stride in (8, 4, 2, 1):
            for i in range(16):
                if i & stride == 0:
                    cur[i], cur[i + stride] = cx(cur[i], cur[i + stride])
    return cur


def _route_kernel(fb_ref, wq_ref, sk_ref, r2_ref, n1_ref, e1_ref, e2_ref, s_scr, top):
    K = PEER_NKEYS
    G = 2 * PEER_HEADS
    H = PEER_HEADS
    tt = fb_ref.shape[0]
    qt = lax.dot_general(wq_ref[...], fb_ref[...], NT_DIMS, preferred_element_type=F32)
    for g in range(G):
        s_scr[g * K:(g + 1) * K, :] = jnp.dot(sk_ref[g], qt[g * K:(g + 1) * K, :].astype(BF16),
                                              preferred_element_type=F32)

    for g in range(G):
        hh, a = divmod(g, 2)
        for c in range(0, tt, LANES):
            lanes = slice(c, c + LANES)
            best = _top16_sorted([s_scr[g * K + r * 8:g * K + (r + 1) * 8, lanes]
                                  for r in range(PEER_TOPK)])
            for p in range(PEER_TOPK):
                row = (a * PEER_TOPK + p) * H + hh
                top[row:row + 1, lanes] = best[p][0:1, :]

    tops_a = [top[p * H:(p + 1) * H, :] for p in range(PEER_TOPK)]
    tops_b = [top[(PEER_TOPK + q) * H:(PEER_TOPK + q + 1) * H, :] for q in range(PEER_TOPK)]
    pairs = _staircase()
    cand = [tops_a[p] + tops_b[q] for (p, q) in pairs]
    cur = list(cand)
    tau = None
    for it in range(PEER_TOPK):
        tau = functools.reduce(jnp.maximum, cur)
        if it + 1 < PEER_TOPK:
            cur = [jnp.where(c == tau, NEG_INF, c) for c in cur]
    c00 = cand[0]
    z = functools.reduce(
        lambda a, b: a + b,
        [jnp.where(c >= tau, jnp.exp(c - c00), 0.0) for c in cand])
    zinv = 1.0 / z

    for hh in range(H):
        s1 = s_scr[(2 * hh) * K:(2 * hh + 1) * K, :]
        s2 = s_scr[(2 * hh + 1) * K:(2 * hh + 2) * K, :]
        tau_h = tau[hh:hh + 1, :]
        desc = [tops_b[q][hh:hh + 1, :] for q in range(PEER_TOPK)]
        cnt = _count_leading(desc, lambda b: s1 + b >= tau_h, s1)
        n1_ref[hh] = _bf16_pair_word(cnt)
        e1_ref[hh] = _bf16_pair_word(jnp.exp(s1 - tops_a[0][hh:hh + 1, :]) * zinv[hh:hh + 1, :])
        e2 = jnp.exp(s2 - tops_b[0][hh:hh + 1, :]).astype(BF16)
        e2_ref[hh] = pltpu.bitcast(e2, jnp.uint32)
        asc = [tops_b[PEER_TOPK - 1 - q][hh:hh + 1, :] for q in range(PEER_TOPK)]
        rank = float(PEER_TOPK) - _count_leading(asc, lambda b: s2 >= b, s2)
        r2_ref[hh] = pltpu.bitcast(rank.astype(BF16), jnp.uint32)


def _route(f, w_q, sub_keys, seq):
    n, d = f.shape
    tt = min(256, seq)
    H, K = PEER_HEADS, PEER_NKEYS
    wq_t = w_q.T.astype(BF16)
    sk = sub_keys.reshape(2 * H, K, PEER_HALF).astype(BF16)
    row_tab = jax.ShapeDtypeStruct((H, K, n), jnp.uint32)
    key_tab = jax.ShapeDtypeStruct((H, K // 2, n), jnp.uint32)
    row_spec = pl.BlockSpec((H, K, tt), lambda t: (0, 0, t))
    key_spec = pl.BlockSpec((H, K // 2, tt), lambda t: (0, 0, t))
    return pl.pallas_call(
        _route_kernel,
        grid=(n // tt,),
        in_specs=[pl.BlockSpec((tt, d), lambda t: (t, 0)),
                  _resident(wq_t.shape), _resident(sk.shape)],
        out_specs=[key_spec, row_spec, row_spec, key_spec],
        out_shape=[key_tab, row_tab, row_tab, key_tab],
        scratch_shapes=[pltpu.VMEM((2 * H * K, tt), F32),
                        pltpu.VMEM((2 * PEER_TOPK * H, tt), F32)],
        compiler_params=_cparams("parallel"),
        name="peer_route",
    )(f, wq_t, sk)


def _gelu(x):
    return 0.5 * x * (1.0 + lax.erf(x * (2.0 ** -0.5)))


def _expert_kernel(fb_ref, u_ref, vt_ref, r2_ref, e2_ref, n1_ref, e1_ref, x1_ref, g_ref, ga_ref,
                   o_ref, acc, h_scr, w_scr, *, rows_per_tile):
    e = pl.program_id(1)
    K = PEER_NKEYS

    @pl.when(e == 0)
    def _():
        acc[...] = jnp.zeros_like(acc)

    tt = fb_ref.shape[0]
    groups = K // BF16_ROWS
    zero = jnp.zeros((BF16_ROWS, tt), BF16)
    fb = fb_ref[...]
    per_chunk = rows_per_tile // EXPERT_ROW_CHUNKS
    for i in range(rows_per_tile):
        if i % per_chunk == 0:
            chunk = slice(i * K, (i + per_chunk) * K)
            h_scr[chunk, :] = lax.dot_general(u_ref[chunk, :], fb, NT_DIMS,
                                              preferred_element_type=F32)
        gate = [None] * groups
        for hh in range(PEER_HEADS):
            n1 = _bf16_rows(n1_ref[hh, i:i + 1, :])
            e1 = _bf16_rows(e1_ref[hh, i:i + 1, :])
            for r in range(groups):
                words = slice(r * BF16_ROWS // 2, (r + 1) * BF16_ROWS // 2)
                r2 = pltpu.bitcast(r2_ref[hh, words, :], BF16)
                e2 = pltpu.bitcast(e2_ref[hh, words, :], BF16)
                term = jnp.where(r2 < n1, e1 * e2, zero)
                gate[r] = term if hh == 0 else gate[r] + term
        for r in range(groups):
            rows = slice(i * K + r * BF16_ROWS, i * K + (r + 1) * BF16_ROWS)
            w_scr[rows, :] = gate[r] * _gelu(h_scr[rows, :].astype(BF16))
    acc[...] += jnp.dot(vt_ref[...], w_scr[...], preferred_element_type=F32)

    @pl.when(e == pl.num_programs(1) - 1)
    def _():
        y = acc[...]
        yn = y * lax.rsqrt(jnp.mean(y * y, axis=0, keepdims=True) + EPS)
        o_ref[...] = x1_ref[...] + ga_ref[0] * (yn.T * g_ref[...])


def _experts(fb, u_bf, vt_bf, r2, n1, e1, e2, x1, post_ffn_g, ga_f, seq):
    n, d = fb.shape
    tt = min(512, seq)
    per_seq = seq // tt
    rows_per_tile = EXPERT_TILE_ROWS
    et = rows_per_tile * PEER_NKEYS
    H, K = PEER_HEADS, PEER_NKEYS
    kern = functools.partial(_expert_kernel, rows_per_tile=rows_per_tile)
    return pl.pallas_call(
        kern,
        grid=(n // tt, PEER_EXPERTS // et),
        in_specs=[pl.BlockSpec((tt, d), lambda t, e: (t, 0)),
                  pl.BlockSpec((et, d), lambda t, e: (e, 0)),
                  pl.BlockSpec((d, et), lambda t, e: (0, e)),
                  pl.BlockSpec((H, K // 2, tt), lambda t, e: (0, 0, t)),
                  pl.BlockSpec((H, K // 2, tt), lambda t, e: (0, 0, t)),
                  pl.BlockSpec((H, rows_per_tile, tt), lambda t, e: (0, e, t)),
                  pl.BlockSpec((H, rows_per_tile, tt), lambda t, e: (0, e, t)),
                  pl.BlockSpec((tt, d), lambda t, e: (t, 0)),
                  pl.BlockSpec((1, d), lambda t, e: (0, 0)),
                  pl.BlockSpec((1, 1, d), lambda t, e: (t // per_seq, 0, 0))],
        out_specs=pl.BlockSpec((tt, d), lambda t, e: (t, 0)),
        out_shape=jax.ShapeDtypeStruct((n, d), F32),
        scratch_shapes=[pltpu.VMEM((d, tt), F32), pltpu.VMEM((et, tt), F32),
                        pltpu.VMEM((et, tt), BF16)],
        compiler_params=_cparams("parallel", "arbitrary"),
        name="peer_experts",
    )(fb, u_bf, vt_bf, r2, e2, n1, e1, x1, post_ffn_g.reshape(1, d), ga_f)


def kernel(x, c, ctx, c_ctx, w_mod, b_mod, pre_mix_g, post_mix_g, pre_ffn_g, post_ffn_g, w_in,
           ret_decay_fwd, ret_decay_bwd, ret_gn_g, diff_lambda, diff_gn_g, w_br_ret, w_br_diff,
           w_out, peer_w_q, peer_sub_keys, peer_u, peer_v):
    batch, seq, d = x.shape
    ctx_len = ctx.shape[1]
    depth = w_mod.shape[0]
    assert depth == 1 and d == D_MODEL
    l = 0
    lam_init = 0.8 - 0.6 * math.exp(-0.3 * l)

    rows = ((batch + 1 + 7) // 8) * 8
    cc = jnp.zeros((rows, d), F32).at[:batch].set(c).at[batch].set(c_ctx)
    mod = _modulation(cc, w_mod[l], b_mod[l])
    sh_a, sc_a, ga_a, sh_f, sc_f, ga_f = [t[:batch, None, :] for t in jnp.split(mod, 6, axis=-1)]
    csh_a, csc_a = [t[batch:batch + 1, None, :] for t in jnp.split(mod, 6, axis=-1)[:2]]

    w_in_bf = w_in[l].astype(BF16)
    x2 = x.reshape(batch * seq, d)
    ctx2 = ctx.reshape(batch * ctx_len, d)

    lat_specs = [(COL_RQ, RET_QK_W, "ret_q"), (COL_RK, RET_QK_W, "ret_k"),
                 (COL_RV, RET_V_W, "plain"), (COL_RG, RET_V_W, "silu"),
                 (COL_DQ, DIFF_W, "diff_q"), (COL_DK, DIFF_W, "diff_k"),
                 (COL_DV, DIFF_W, "plain"), (COL_GR, D_MODEL, "sigmoid"),
                 (COL_GD, D_MODEL, "sigmoid")]
    tables = (_rope_tables(seq, RET_DK), _rope_tables(seq, DIFF_D))
    rq, rk, rv, rg, dq, dk, dv, g_r, g_d = _inproj(
        x2, pre_mix_g[l], sc_a, sh_a, w_in_bf, lat_specs, seq, tables)

    ctx_specs = [(COL_RK, RET_QK_W, "plain"), (COL_RV, RET_V_W, "plain"),
                 (COL_DK, DIFF_W, "plain"), (COL_DV, DIFF_W, "plain")]
    rk_c, rv_c, dk_c, dv_c = _inproj(ctx2, pre_mix_g[l], csc_a, csh_a, w_in_bf, ctx_specs,
                                     ctx_len, None)

    z_ret = _retention(rq, rk, rv, rg, rk_c, rv_c, ret_decay_fwd[l], ret_decay_bwd[l],
                       ret_gn_g[l], batch, seq, ctx_len)
    z_diff = _diff_attention(dq, dk, dv, dk_c, dv_c, diff_lambda[l], diff_gn_g[l], lam_init,
                             batch, seq, ctx_len)
    x1, fb = _merge(x2, z_ret, z_diff, g_r, g_d, w_br_ret[l], w_br_diff[l], w_out[l],
                   post_mix_g[l], pre_ffn_g[l], ga_a, sh_f, sc_f, seq)

    r2, n1, e1, e2 = _route(fb, peer_w_q[l], peer_sub_keys[l], seq)
    out = _experts(fb, peer_u[l].astype(BF16), peer_v[l].T.astype(BF16), r2, n1, e1, e2, x1,
                   post_ffn_g[l], ga_f, seq)
    return out.reshape(batch, seq, d)
```

```python
import functools
import math

import jax
import jax.numpy as jnp
from jax import lax
from jax.experimental import pallas as pl
from jax.experimental.pallas import tpu as pltpu

F32 = jnp.float32
BF16 = jnp.bfloat16

D_MODEL = 1024
GRID_W = 64
EPS = 1e-6
ROPE_BASE = 10000.0

RET_HEADS = 4
RET_DK = 128
RET_DV = 256
RET_CHUNK = 128
RET_UNROLL = 4
RET_QK_W = RET_HEADS * RET_DK
RET_V_W = RET_HEADS * RET_DV

DIFF_HEADS = 8
DIFF_D = 64
DIFF_W = DIFF_HEADS * 2 * DIFF_D
MERGE_SUBTILES = 2
DIFF_Q_TILE = 2048
DIFF_SUB_ROWS = 128

PEER_HEADS = 8
PEER_NKEYS = 128
PEER_EXPERTS = PEER_NKEYS * PEER_NKEYS
PEER_HALF = 128
PEER_TOPK = 16

COL_RQ = 0
COL_RK = COL_RQ + RET_QK_W
COL_RV = COL_RK + RET_QK_W
COL_RG = COL_RV + RET_V_W
COL_DQ = COL_RG + RET_V_W
COL_DK = COL_DQ + DIFF_W
COL_DV = COL_DK + DIFF_W
COL_GR = COL_DV + DIFF_W
COL_GD = COL_GR + D_MODEL
IN_COLS = COL_GD + D_MODEL

LANES = 128
BF16_ROWS = 16
EXPERT_TILE_ROWS = 32
EXPERT_ROW_CHUNKS = 16
VMEM_LIMIT = 56 << 20
EXPERT_VMEM_LIMIT = 58 << 20

NEG_INF = float("-inf")
LOG2_E = math.log2(math.e)

NT_DIMS = (((1,), (1,)), ((), ()))
TN_DIMS = (((0,), (0,)), ((), ()))


def _cparams(*sem, vmem_limit=VMEM_LIMIT):
    return pltpu.CompilerParams(dimension_semantics=sem, vmem_limit_bytes=vmem_limit)


def _resident(shape):
    nd = len(shape)
    return pl.BlockSpec(shape, lambda *_: (0,) * nd, pipeline_mode=pl.Buffered(1))


def _mod_kernel(c_ref, w_ref, b_ref, o_ref):
    c = c_ref[...]
    s = c * jax.nn.sigmoid(c)
    o_ref[...] = jnp.dot(s, w_ref[...], preferred_element_type=F32,
                         precision=lax.Precision.HIGHEST) + b_ref[...]


def _modulation(cc, w, b):
    rows, d = cc.shape
    n = w.shape[1]
    tn = 768
    return pl.pallas_call(
        _mod_kernel,
        grid=(n // tn,),
        in_specs=[pl.BlockSpec((rows, d), lambda j: (0, 0)),
                  pl.BlockSpec((d, tn), lambda j: (0, j)),
                  pl.BlockSpec((1, tn), lambda j: (0, j))],
        out_specs=pl.BlockSpec((rows, tn), lambda j: (0, j)),
        out_shape=jax.ShapeDtypeStruct((rows, n), F32),
        compiler_params=_cparams("arbitrary"),
        name="modulation",
    )(cc, w, b.reshape(1, n))


def _rope_tables(seq, head_dim):
    rows = seq // GRID_W
    row = jnp.repeat(jnp.arange(rows, dtype=F32), GRID_W)
    col = jnp.tile(jnp.arange(GRID_W, dtype=F32), rows)
    half = head_dim // 2
    pair = half // 2
    lane = jnp.arange(LANES)
    d = lane % head_dim
    inv = ROPE_BASE ** (-jnp.arange(pair, dtype=F32) / pair)
    freq = inv[d % pair]
    pos = jnp.where((d < half)[None, :], row[:, None], col[:, None])
    ang = pos * freq[None, :]
    cos = jnp.cos(ang)
    sin = jnp.sin(ang)
    first = ((d % half) < pair)[None, :]
    sin_a = jnp.where(first, -sin, 0.0)
    sin_b = jnp.where(first, 0.0, sin)
    return cos, sin_a, sin_b, pair


def _rope(acc, cos, sin_a, sin_b, pair):
    up = pltpu.roll(acc, LANES - pair, 1)
    dn = pltpu.roll(acc, pair, 1)
    return acc * cos + up * sin_a + dn * sin_b


def _inproj_kernel(*refs, specs, rope, ret_pair, diff_pair):
    if rope:
        (x_ref, g_ref, sc_ref, sh_ref, w_ref,
         rc_ref, ra_ref, rb_ref, dc_ref, da_ref, db_ref) = refs[:11]
        out_refs = refs[11:]
    else:
        x_ref, g_ref, sc_ref, sh_ref, w_ref = refs[:5]
        out_refs = refs[5:]
    x = x_ref[...]
    y = x * lax.rsqrt(jnp.mean(x * x, axis=-1, keepdims=True) + EPS)
    u = (y * g_ref[...]) * (1.0 + sc_ref[0]) + sh_ref[0]
    ub = u.astype(BF16)
    for (col0, width, kind), o_ref in zip(specs, out_refs):
        for c in range(0, width, 512):
            cw = min(512, width - c)
            acc = jnp.dot(ub, w_ref[:, col0 + c:col0 + c + cw], preferred_element_type=F32)
            if kind in ("ret_q", "ret_k", "diff_q", "diff_k"):
                for l in range(0, cw, LANES):
                    a = acc[:, l:l + LANES]
                    if kind.startswith("ret"):
                        r = _rope(a, rc_ref[...], ra_ref[...], rb_ref[...], ret_pair)
                    else:
                        r = _rope(a, dc_ref[...], da_ref[...], db_ref[...], diff_pair)
                    if kind == "ret_q":
                        r = r * (RET_DK ** -0.5)
                    elif kind == "diff_q":
                        r = r * (DIFF_D ** -0.5 * LOG2_E)
                    o_ref[:, c + l:c + l + LANES] = r.astype(BF16)
            elif kind == "silu":
                o_ref[:, c:c + cw] = (acc * jax.nn.sigmoid(acc)).astype(BF16)
            elif kind == "sigmoid":
                o_ref[:, c:c + cw] = jax.nn.sigmoid(acc).astype(BF16)
            else:
                o_ref[:, c:c + cw] = acc.astype(BF16)


def _inproj(x2, gain, scale, shift, w_bf, specs, seq, tables):
    n, d = x2.shape
    tm = min(512, seq)
    assert seq % tm == 0 and n % seq == 0
    per_seq = seq // tm
    nb = scale.shape[0]
    if nb == 1:
        mod_map = lambda i: (0, 0, 0)
    else:
        mod_map = lambda i: (i // per_seq, 0, 0)
    rope = tables is not None
    in_specs = [pl.BlockSpec((tm, d), lambda i: (i, 0)),
                pl.BlockSpec((1, d), lambda i: (0, 0)),
                pl.BlockSpec((1, 1, d), mod_map),
                pl.BlockSpec((1, 1, d), mod_map),
                _resident(w_bf.shape)]
    args = [x2, gain.reshape(1, d), scale, shift, w_bf]
    ret_pair = diff_pair = 0
    if rope:
        (rc, ra, rb, ret_pair), (dc, da, db, diff_pair) = tables
        tab_spec = pl.BlockSpec((tm, LANES), lambda i: (i % per_seq, 0))
        in_specs += [tab_spec] * 6
        args += [rc, ra, rb, dc, da, db]
    out_specs = [pl.BlockSpec((tm, w), lambda i: (i, 0)) for (_, w, _) in specs]
    out_shape = [jax.ShapeDtypeStruct((n, w), BF16) for (_, w, _) in specs]
    kern = functools.partial(_inproj_kernel, specs=tuple(specs), rope=rope,
                             ret_pair=ret_pair, diff_pair=diff_pair)
    return pl.pallas_call(
        kern, grid=(n // tm,), in_specs=in_specs, out_specs=out_specs, out_shape=out_shape,
        compiler_params=_cparams("parallel"),
        name="inproj_rope" if rope else "inproj_ctx",
    )(*args)


def _ret_kernel(sdec_ref, q_ref, k_ref, v_ref, g_ref, kc_ref, vc_ref, m_ref, dec_ref, cdec_ref,
                gn_ref, o_ref, ybuf, kv, st, sf, sb, *, n_chunks):
    C = RET_CHUNK
    DK = RET_DK
    h = pl.program_id(1)
    sdec_f = sdec_ref[2 * h]
    sdec_b = sdec_ref[2 * h + 1]

    kc = kc_ref[...].astype(F32)
    vc = vc_ref[...]
    sf[...] = lax.dot_general((kc * cdec_ref[0, 0]).astype(BF16), vc, TN_DIMS,
                              preferred_element_type=F32)
    sb[...] = lax.dot_general((kc * cdec_ref[0, 1]).astype(BF16), vc, TN_DIMS,
                              preferred_element_type=F32)

    dmat = m_ref[0]
    qdec_f = dec_ref[0, 0]
    kdec_f = dec_ref[0, 1]
    qdec_b = dec_ref[0, 2]
    kdec_b = dec_ref[0, 3]
    gn = gn_ref[...]

    def intra(i, carry):
        r = pl.multiple_of(i * C, C)
        q = q_ref[pl.ds(r, C), :]
        k = k_ref[pl.ds(r, C), :]
        v = v_ref[pl.ds(r, C), :]
        kf = k.astype(F32)
        att = lax.dot_general(q, k, NT_DIMS, preferred_element_type=F32) * dmat
        ybuf[pl.ds(r, C), :] = jnp.dot(att.astype(BF16), v, preferred_element_type=F32)
        kk = jnp.concatenate([(kf * kdec_f).astype(BF16), (kf * kdec_b).astype(BF16)], axis=1)
        kv[i] = lax.dot_general(kk, v, TN_DIMS, preferred_element_type=F32)
        return carry

    lax.fori_loop(0, n_chunks, intra, 0, unroll=RET_UNROLL)

    def scan(t, carry):
        i = t
        st[i, :DK, :] = sf[...].astype(BF16)
        sf[...] = sdec_f * sf[...] + kv[i, :DK, :]
        j = n_chunks - 1 - t
        st[j, DK:, :] = sb[...].astype(BF16)
        sb[...] = sdec_b * sb[...] + kv[j, DK:, :]
        return carry

    lax.fori_loop(0, n_chunks, scan, 0)

    def cross(i, carry):
        r = pl.multiple_of(i * C, C)
        qf = q_ref[pl.ds(r, C), :].astype(F32)
        qq = jnp.concatenate([(qf * qdec_f).astype(BF16), (qf * qdec_b).astype(BF16)], axis=1)
        y = ybuf[pl.ds(r, C), :] + jnp.dot(qq, st[i], preferred_element_type=F32)
        mu = jnp.mean(y, axis=-1, keepdims=True)
        yc = y - mu
        var = jnp.mean(yc * yc, axis=-1, keepdims=True)
        yn = yc * lax.rsqrt(var + EPS) * gn
        o_ref[pl.ds(r, C), :] = (g_ref[pl.ds(r, C), :].astype(F32) * yn).astype(BF16)
        return carry

    lax.fori_loop(0, n_chunks, cross, 0, unroll=RET_UNROLL)


def _retention(ret_q, ret_k, ret_v, ret_g, ctx_rk, ctx_rv, dec_f, dec_b, gn_g, batch, seq, ctx_len):
    C = RET_CHUNK
    H = RET_HEADS
    lg_f = jnp.log1p(-jnp.exp2(dec_f.astype(F32)))
    lg_b = jnp.log1p(-jnp.exp2(dec_b.astype(F32)))
    idx = jnp.arange(C, dtype=F32)
    dist = idx[:, None] - idx[None, :]
    dmat = jnp.where(dist[None] >= 0,
                     jnp.exp(lg_f[:, None, None] * jnp.maximum(dist, 0.0)[None]),
                     jnp.exp(lg_b[:, None, None] * jnp.maximum(-dist, 0.0)[None]))
    qdec_f = jnp.exp(lg_f[:, None] * (idx + 1.0))
    kdec_f = jnp.exp(lg_f[:, None] * (C - 1.0 - idx))
    qdec_b = jnp.exp(lg_b[:, None] * (C - idx))
    kdec_b = jnp.exp(lg_b[:, None] * idx)
    dec = jnp.stack([qdec_f, kdec_f, qdec_b, kdec_b], axis=1)
    dec = jnp.broadcast_to(dec[..., None], (H, 4, C, RET_DK))
    cidx = jnp.arange(ctx_len, dtype=F32)
    cdec = jnp.stack([jnp.exp(lg_f[:, None] * (ctx_len - 1.0 - cidx)),
                      jnp.exp(lg_b[:, None] * cidx)], axis=1)
    cdec = jnp.broadcast_to(cdec[..., None], (H, 2, ctx_len, RET_DK))
    sdec = jnp.stack([jnp.exp(lg_f * C), jnp.exp(lg_b * C)], axis=1).reshape(2 * H)

    n_chunks = seq // C
    kern = functools.partial(_ret_kernel, n_chunks=n_chunks)
    return pl.pallas_call(
        kern,
        grid=(batch, H),
        in_specs=[pl.BlockSpec(memory_space=pltpu.SMEM),
                  pl.BlockSpec((seq, RET_DK), lambda b, h: (b, h)),
                  pl.BlockSpec((seq, RET_DK), lambda b, h: (b, h)),
                  pl.BlockSpec((seq, RET_DV), lambda b, h: (b, h)),
                  pl.BlockSpec((seq, RET_DV), lambda b, h: (b, h)),
                  pl.BlockSpec((ctx_len, RET_DK), lambda b, h: (b, h)),
                  pl.BlockSpec((ctx_len, RET_DV), lambda b, h: (b, h)),
                  pl.BlockSpec((1, C, C), lambda b, h: (h, 0, 0)),
                  pl.BlockSpec((1, 4, C, RET_DK), lambda b, h: (h, 0, 0, 0)),
                  pl.BlockSpec((1, 2, ctx_len, RET_DK), lambda b, h: (h, 0, 0, 0)),
                  pl.BlockSpec((1, RET_DV), lambda b, h: (0, h))],
        out_specs=pl.BlockSpec((seq, RET_DV), lambda b, h: (b, h)),
        out_shape=jax.ShapeDtypeStruct((batch * seq, RET_V_W), BF16),
        scratch_shapes=[pltpu.VMEM((seq, RET_DV), F32),
                        pltpu.VMEM((n_chunks, 2 * RET_DK, RET_DV), F32),
                        pltpu.VMEM((n_chunks, 2 * RET_DK, RET_DV), BF16),
                        pltpu.VMEM((RET_DK, RET_DV), F32),
                        pltpu.VMEM((RET_DK, RET_DV), F32)],
        compiler_params=_cparams("parallel", "arbitrary"),
        name="retention",
    )(sdec, ret_q, ret_k, ret_v, ret_g, ctx_rk, ctx_rv, dmat, dec, cdec,
      gn_g.reshape(1, RET_V_W))


def _diff_kernel(q_ref, kl_ref, kc_ref, vl_ref, vc_ref, lp_ref, gn_ref, o_ref, *, lam_init):
    kl = kl_ref[...]
    kc = kc_ref[...]
    tq, hw = q_ref.shape
    vl = jnp.concatenate([vl_ref[...], jnp.ones(vl_ref.shape, BF16)], axis=1)
    vc = jnp.concatenate([vc_ref[...], jnp.ones(vc_ref.shape, BF16)], axis=1)
    lp = lp_ref[...]
    lam = (jnp.exp(jnp.sum(lp[0:1] * lp[1:2], axis=-1, keepdims=True))
           - jnp.exp(jnp.sum(lp[2:3] * lp[3:4], axis=-1, keepdims=True)) + lam_init)

    def branch(qm):
        sl = lax.dot_general(qm, kl, NT_DIMS, preferred_element_type=F32)
        sc = lax.dot_general(qm, kc, NT_DIMS, preferred_element_type=F32)
        m = jnp.maximum(jnp.max(sl, axis=-1, keepdims=True), jnp.max(sc, axis=-1, keepdims=True))
        pl_ = jnp.exp2(sl - m)
        pc = jnp.exp2(sc - m)
        o = (jnp.dot(pl_.astype(BF16), vl, preferred_element_type=F32)
             + jnp.dot(pc.astype(BF16), vc, preferred_element_type=F32))
        return o[:, :hw] / o[:, hw:]

    sub = DIFF_SUB_ROWS
    for t in range(tq // sub):
        rows = slice(t * sub, (t + 1) * sub)
        q = q_ref[rows, :]
        lane = lax.broadcasted_iota(jnp.int32, q.shape, 1)
        zero = jnp.zeros_like(q)
        o1 = branch(jnp.where(lane < DIFF_D, q, zero))
        o2 = branch(jnp.where(lane >= DIFF_D, q, zero))
        o = o1 - lam * o2
        on = o * lax.rsqrt(jnp.mean(o * o, axis=-1, keepdims=True) + EPS) * (1.0 - lam_init)
        o_ref[rows, :] = (on * gn_ref[...]).astype(BF16)


def _diff_attention(dq, dk, dv, ctx_dk, ctx_dv, lam_params, gn_g, lam_init, batch, seq, ctx_len):
    tq = min(DIFF_Q_TILE, seq)
    nq = seq // tq
    hw = 2 * DIFF_D
    kern = functools.partial(_diff_kernel, lam_init=lam_init)
    return pl.pallas_call(
        kern,
        grid=(batch, DIFF_HEADS, nq),
        in_specs=[pl.BlockSpec((tq, hw), lambda b, h, i: (b * nq + i, h)),
                  pl.BlockSpec((seq, hw), lambda b, h, i: (b, h)),
                  pl.BlockSpec((ctx_len, hw), lambda b, h, i: (b, h)),
                  pl.BlockSpec((seq, hw), lambda b, h, i: (b, h)),
                  pl.BlockSpec((ctx_len, hw), lambda b, h, i: (b, h)),
                  pl.BlockSpec((4, DIFF_D), lambda b, h, i: (0, 0)),
                  pl.BlockSpec((1, hw), lambda b, h, i: (0, h))],
        out_specs=pl.BlockSpec((tq, hw), lambda b, h, i: (b * nq + i, h)),
        out_shape=jax.ShapeDtypeStruct((batch * seq, DIFF_W), BF16),
        compiler_params=_cparams("parallel", "parallel", "arbitrary"),
        name="diff_attention",
    )(dq, dk, ctx_dk, dv, ctx_dv, lam_params, gn_g.reshape(1, DIFF_W))


def _merge_kernel(x_ref, zr_ref, zd_ref, gr_ref, gd_ref, wr_ref, wd_ref, wo_ref,
                  pmg_ref, pfg_ref, ga_ref, shf_ref, scf_ref, x1_ref, f_ref):
    sub = x_ref.shape[0] // MERGE_SUBTILES
    for t in range(MERGE_SUBTILES):
        rows = slice(t * sub, (t + 1) * sub)
        p_ret = jnp.dot(zr_ref[rows, :], wr_ref[...], preferred_element_type=F32)
        p_diff = jnp.dot(zd_ref[rows, :], wd_ref[...], preferred_element_type=F32)
        m = gr_ref[rows, :].astype(F32) * p_ret + gd_ref[rows, :].astype(F32) * p_diff
        mix = jnp.dot(m.astype(BF16), wo_ref[...], preferred_element_type=F32)
        mixn = mix * lax.rsqrt(jnp.mean(mix * mix, axis=-1, keepdims=True) + EPS) * pmg_ref[...]
        x1 = x_ref[rows, :] + ga_ref[0] * mixn
        x1_ref[rows, :] = x1
        fn = x1 * lax.rsqrt(jnp.mean(x1 * x1, axis=-1, keepdims=True) + EPS) * pfg_ref[...]
        f_ref[rows, :] = (fn * (1.0 + scf_ref[0]) + shf_ref[0]).astype(BF16)


def _merge(x2, z_ret, z_diff, g_r, g_d, w_br_ret, w_br_diff, w_out, post_mix_g, pre_ffn_g,
           ga_a, sh_f, sc_f, seq):
    n, d = x2.shape
    tm = min(512, seq)
    per_seq = seq // tm
    row = lambda i: (i, 0)
    mod = lambda i: (i // per_seq, 0, 0)
    vec = pl.BlockSpec((1, d), lambda i: (0, 0))
    return pl.pallas_call(
        _merge_kernel,
        grid=(n // tm,),
        in_specs=[pl.BlockSpec((tm, d), row)] * 5
                 + [_resident((d, d))] * 3
                 + [vec, vec]
                 + [pl.BlockSpec((1, 1, d), mod)] * 3,
        out_specs=[pl.BlockSpec((tm, d), row), pl.BlockSpec((tm, d), row)],
        out_shape=[jax.ShapeDtypeStruct((n, d), F32), jax.ShapeDtypeStruct((n, d), BF16)],
        compiler_params=_cparams("parallel"),
        name="merge",
    )(x2, z_ret, z_diff, g_r, g_d, w_br_ret.astype(BF16), w_br_diff.astype(BF16),
      w_out.astype(BF16), post_mix_g.reshape(1, d), pre_ffn_g.reshape(1, d), ga_a, sh_f, sc_f)


def _staircase():
    return [(p, q) for p in range(PEER_TOPK) for q in range(PEER_TOPK)
            if (p + 1) * (q + 1) <= PEER_TOPK]


def _bf16_pair_word(x):
    hi = pltpu.bitcast(x.astype(BF16).astype(F32), jnp.uint32)
    return hi | (hi >> 16)


def _bf16_rows(word_row):
    w = jnp.broadcast_to(word_row, (BF16_ROWS // 2, word_row.shape[1]))
    return pltpu.bitcast(w, BF16)


def _count_leading(rows, test, like):
    assert len(rows) == 16
    count = jnp.zeros_like(like)
    decisions = []
    for width in (8, 4, 2, 1):
        cands = [rows[base + width - 1] for base in range(0, 16, 2 * width)]
        for bit in reversed(decisions):
            cands = [jnp.where(bit, hi, lo) for lo, hi in zip(cands[0::2], cands[1::2])]
        passed = test(cands[0])
        decisions.append(passed)
        count = count + jnp.where(passed, float(width), 0.0)
    return jnp.where(test(rows[15]), 16.0, count)


def _sort16_pairs():
    pairs = []

    def merge(lo, hi, r):
        step = r * 2
        if step < hi - lo:
            merge(lo, hi, step)
            merge(lo + r, hi, step)
            pairs.extend((i, i + r) for i in range(lo + r, hi - r, step))
        else:
            pairs.append((lo, lo + r))

    def sort(lo, hi):
        if hi - lo >= 1:
            mid = lo + (hi - lo) // 2
            sort(lo, mid)
            sort(mid + 1, hi)
            merge(lo, hi, 1)

    sort(0, 15)
    return pairs


def _top16_sorted(slabs):
    cx = lambda a, b: (jnp.maximum(a, b), jnp.minimum(a, b))
    cur = list(slabs)
    for i, j in _sort16_pairs():
        cur[i], cur[j] = cx(cur[i], cur[j])
    for shift in (4, 2, 1):
        other = [pltpu.roll(x, shift, 0) for x in cur]
        cur = [jnp.maximum(cur[r], other[15 - r]) for r in range(16)]
        for stride in (8, 4, 2, 1):
            for i in range(16):
                if i & stride == 0:
                    cur[i], cur[i + stride] = cx(cur[i], cur[i + stride])
    return cur


def _route_kernel(fb_ref, wq_ref, sk_ref, r2_ref, n1_ref, e1_ref, e2_ref, s_scr, top):
    K = PEER_NKEYS
    G = 2 * PEER_HEADS
    H = PEER_HEADS
    tt = fb_ref.shape[0]
    qt = lax.dot_general(wq_ref[...], fb_ref[...], NT_DIMS, preferred_element_type=F32)
    for g in range(G):
        s_scr[g * K:(g + 1) * K, :] = jnp.dot(sk_ref[g], qt[g * K:(g + 1) * K, :].astype(BF16),
                                              preferred_element_type=F32)

    for g in range(G):
        hh, a = divmod(g, 2)
        for c in range(0, tt, LANES):
            lanes = slice(c, c + LANES)
            best = _top16_sorted([s_scr[g * K + r * 8:g * K + (r + 1) * 8, lanes]
                                  for r in range(PEER_TOPK)])
            for p in range(PEER_TOPK):
                row = (a * PEER_TOPK + p) * H + hh
                top[row:row + 1, lanes] = best[p][0:1, :]

    tops_a = [top[p * H:(p + 1) * H, :] for p in range(PEER_TOPK)]
    tops_b = [top[(PEER_TOPK + q) * H:(PEER_TOPK + q + 1) * H, :] for q in range(PEER_TOPK)]
    pairs = _staircase()
    cand = [tops_a[p] + tops_b[q] for (p, q) in pairs]
    cur = list(cand)
    tau = None
    for it in range(PEER_TOPK):
        tau = functools.reduce(jnp.maximum, cur)
        if it + 1 < PEER_TOPK:
            cur = [jnp.where(c == tau, NEG_INF, c) for c in cur]
    c00 = cand[0]
    z = functools.reduce(
        lambda a, b: a + b,
        [jnp.where(c >= tau, jnp.exp(c - c00), 0.0) for c in cand])
    zinv = 1.0 / z

    for hh in range(H):
        s1 = s_scr[(2 * hh) * K:(2 * hh + 1) * K, :]
        s2 = s_scr[(2 * hh + 1) * K:(2 * hh + 2) * K, :]
        tau_h = tau[hh:hh + 1, :]
        desc = [tops_b[q][hh:hh + 1, :] for q in range(PEER_TOPK)]
        cnt = _count_leading(desc, lambda b: s1 + b >= tau_h, s1)
        n1_ref[hh] = _bf16_pair_word(cnt)
        e1_ref[hh] = _bf16_pair_word(jnp.exp(s1 - tops_a[0][hh:hh + 1, :]) * zinv[hh:hh + 1, :])
        e2 = jnp.exp(s2 - tops_b[0][hh:hh + 1, :]).astype(BF16)
        e2_ref[hh] = pltpu.bitcast(e2, jnp.uint32)
        asc = [tops_b[PEER_TOPK - 1 - q][hh:hh + 1, :] for q in range(PEER_TOPK)]
        rank = float(PEER_TOPK) - _count_leading(asc, lambda b: s2 >= b, s2)
        r2_ref[hh] = pltpu.bitcast(rank.astype(BF16), jnp.uint32)


def _route(f, w_q, sub_keys, seq):
    n, d = f.shape
    tt = min(256, seq)
    H, K = PEER_HEADS, PEER_NKEYS
    wq_t = w_q.T.astype(BF16)
    sk = sub_keys.reshape(2 * H, K, PEER_HALF).astype(BF16)
    row_tab = jax.ShapeDtypeStruct((H, K, n), jnp.uint32)
    key_tab = jax.ShapeDtypeStruct((H, K // 2, n), jnp.uint32)
    row_spec = pl.BlockSpec((H, K, tt), lambda t: (0, 0, t))
    key_spec = pl.BlockSpec((H, K // 2, tt), lambda t: (0, 0, t))
    return pl.pallas_call(
        _route_kernel,
        grid=(n // tt,),
        in_specs=[pl.BlockSpec((tt, d), lambda t: (t, 0)),
                  _resident(wq_t.shape), _resident(sk.shape)],
        out_specs=[key_spec, row_spec, row_spec, key_spec],
        out_shape=[key_tab, row_tab, row_tab, key_tab],
        scratch_shapes=[pltpu.VMEM((2 * H * K, tt), F32),
                        pltpu.VMEM((2 * PEER_TOPK * H, tt), F32)],
        compiler_params=_cparams("parallel"),
        name="peer_route",
    )(f, wq_t, sk)


def _gelu(x):
    return 0.5 * x * (1.0 + lax.erf(x * (2.0 ** -0.5)))


def _expert_kernel(fb_ref, u_ref, vt_ref, r2_ref, e2_ref, n1_ref, e1_ref, x1_ref, g_ref, ga_ref,
                   o_ref, acc, h_scr, w_scr, *, rows_per_tile):
    e = pl.program_id(1)
    K = PEER_NKEYS

    @pl.when(e == 0)
    def _():
        acc[...] = jnp.zeros_like(acc)

    tt = fb_ref.shape[0]
    groups = K // BF16_ROWS
    zero = jnp.zeros((BF16_ROWS, tt), BF16)
    fb = fb_ref[...]
    per_chunk = rows_per_tile // EXPERT_ROW_CHUNKS
    for i in range(rows_per_tile):
        if i % per_chunk == 0:
            chunk = slice(i * K, (i + per_chunk) * K)
            h_scr[chunk, :] = lax.dot_general(u_ref[chunk, :], fb, NT_DIMS,
                                              preferred_element_type=F32).astype(BF16)
        gate = [None] * groups
        for hh in range(PEER_HEADS):
            n1 = _bf16_rows(n1_ref[hh, i:i + 1, :])
            e1 = _bf16_rows(e1_ref[hh, i:i + 1, :])
            for r in range(groups):
                words = slice(r * BF16_ROWS // 2, (r + 1) * BF16_ROWS // 2)
                r2 = pltpu.bitcast(r2_ref[hh, words, :], BF16)
                e2 = pltpu.bitcast(e2_ref[hh, words, :], BF16)
                term = jnp.where(r2 < n1, e1 * e2, zero)
                gate[r] = term if hh == 0 else gate[r] + term
        for r in range(groups):
            rows = slice(i * K + r * BF16_ROWS, i * K + (r + 1) * BF16_ROWS)
            w_scr[rows, :] = gate[r] * _gelu(h_scr[rows, :])
    acc[...] += jnp.dot(vt_ref[...], w_scr[...], preferred_element_type=F32)

    @pl.when(e == pl.num_programs(1) - 1)
    def _():
        y = acc[...]
        yn = y * lax.rsqrt(jnp.mean(y * y, axis=0, keepdims=True) + EPS)
        o_ref[...] = x1_ref[...] + ga_ref[0] * (yn.T * g_ref[...])


def _experts(fb, u_bf, vt_bf, r2, n1, e1, e2, x1, post_ffn_g, ga_f, seq):
    n, d = fb.shape
    tt = min(512, seq)
    per_seq = seq // tt
    rows_per_tile = EXPERT_TILE_ROWS
    et = rows_per_tile * PEER_NKEYS
    H, K = PEER_HEADS, PEER_NKEYS
    kern = functools.partial(_expert_kernel, rows_per_tile=rows_per_tile)
    token_tile = pl.BlockSpec((tt, d), lambda t, e: (t, 0), pipeline_mode=pl.Buffered(1))
    return pl.pallas_call(
        kern,
        grid=(n // tt, PEER_EXPERTS // et),
        in_specs=[token_tile,
                  pl.BlockSpec((et, d), lambda t, e: (e, 0)),
                  pl.BlockSpec((d, et), lambda t, e: (0, e)),
                  pl.BlockSpec((H, K // 2, tt), lambda t, e: (0, 0, t)),
                  pl.BlockSpec((H, K // 2, tt), lambda t, e: (0, 0, t)),
                  pl.BlockSpec((H, rows_per_tile, tt), lambda t, e: (0, e, t)),
                  pl.BlockSpec((H, rows_per_tile, tt), lambda t, e: (0, e, t)),
                  token_tile,
                  pl.BlockSpec((1, d), lambda t, e: (0, 0)),
                  pl.BlockSpec((1, 1, d), lambda t, e: (t // per_seq, 0, 0))],
        out_specs=pl.BlockSpec((tt, d), lambda t, e: (t, 0)),
        out_shape=jax.ShapeDtypeStruct((n, d), F32),
        scratch_shapes=[pltpu.VMEM((d, tt), F32), pltpu.VMEM((et, tt), BF16),
                        pltpu.VMEM((et, tt), BF16)],
        compiler_params=_cparams("parallel", "arbitrary", vmem_limit=EXPERT_VMEM_LIMIT),
        name="peer_experts",
    )(fb, u_bf, vt_bf, r2, e2, n1, e1, x1, post_ffn_g.reshape(1, d), ga_f)


def kernel(x, c, ctx, c_ctx, w_mod, b_mod, pre_mix_g, post_mix_g, pre_ffn_g, post_ffn_g, w_in,
           ret_decay_fwd, ret_decay_bwd, ret_gn_g, diff_lambda, diff_gn_g, w_br_ret, w_br_diff,
           w_out, peer_w_q, peer_sub_keys, peer_u, peer_v):
    batch, seq, d = x.shape
    ctx_len = ctx.shape[1]
    depth = w_mod.shape[0]
    assert depth == 1 and d == D_MODEL
    l = 0
    lam_init = 0.8 - 0.6 * math.exp(-0.3 * l)

    rows = ((batch + 1 + 7) // 8) * 8
    cc = jnp.zeros((rows, d), F32).at[:batch].set(c).at[batch].set(c_ctx)
    mod = _modulation(cc, w_mod[l], b_mod[l])
    sh_a, sc_a, ga_a, sh_f, sc_f, ga_f = [t[:batch, None, :] for t in jnp.split(mod, 6, axis=-1)]
    csh_a, csc_a = [t[batch:batch + 1, None, :] for t in jnp.split(mod, 6, axis=-1)[:2]]

    w_in_bf = w_in[l].astype(BF16)
    x2 = x.reshape(batch * seq, d)
    ctx2 = ctx.reshape(batch * ctx_len, d)

    lat_specs = [(COL_RQ, RET_QK_W, "ret_q"), (COL_RK, RET_QK_W, "ret_k"),
                 (COL_RV, RET_V_W, "plain"), (COL_RG, RET_V_W, "silu"),
                 (COL_DQ, DIFF_W, "diff_q"), (COL_DK, DIFF_W, "diff_k"),
                 (COL_DV, DIFF_W, "plain"), (COL_GR, D_MODEL, "sigmoid"),
                 (COL_GD, D_MODEL, "sigmoid")]
    tables = (_rope_tables(seq, RET_DK), _rope_tables(seq, DIFF_D))
    rq, rk, rv, rg, dq, dk, dv, g_r, g_d = _inproj(
        x2, pre_mix_g[l], sc_a, sh_a, w_in_bf, lat_specs, seq, tables)

    ctx_specs = [(COL_RK, RET_QK_W, "plain"), (COL_RV, RET_V_W, "plain"),
                 (COL_DK, DIFF_W, "plain"), (COL_DV, DIFF_W, "plain")]
    rk_c, rv_c, dk_c, dv_c = _inproj(ctx2, pre_mix_g[l], csc_a, csh_a, w_in_bf, ctx_specs,
                                     ctx_len, None)

    z_ret = _retention(rq, rk, rv, rg, rk_c, rv_c, ret_decay_fwd[l], ret_decay_bwd[l],
                       ret_gn_g[l], batch, seq, ctx_len)
    z_diff = _diff_attention(dq, dk, dv, dk_c, dv_c, diff_lambda[l], diff_gn_g[l], lam_init,
                             batch, seq, ctx_len)
    x1, fb = _merge(x2, z_ret, z_diff, g_r, g_d, w_br_ret[l], w_br_diff[l], w_out[l],
                   post_mix_g[l], pre_ffn_g[l], ga_a, sh_f, sc_f, seq)

    r2, n1, e1, e2 = _route(fb, peer_w_q[l], peer_sub_keys[l], seq)
    out = _experts(fb, peer_u[l].astype(BF16), peer_v[l].T.astype(BF16), r2, n1, e1, e2, x1,
                   post_ffn_g[l], ga_f, seq)
    return out.reshape(batch, seq, d)
```

```python
import functools
import math

import jax
import jax.numpy as jnp
from jax import lax
from jax.experimental import pallas as pl
from jax.experimental.pallas import tpu as pltpu

F32 = jnp.float32
BF16 = jnp.bfloat16

D_MODEL = 1024
GRID_W = 64
EPS = 1e-6
ROPE_BASE = 10000.0

RET_HEADS = 4
RET_DK = 128
RET_DV = 256
RET_CHUNK = 128
RET_UNROLL = 16
RET_QK_W = RET_HEADS * RET_DK
RET_V_W = RET_HEADS * RET_DV

DIFF_HEADS = 8
DIFF_D = 64
DIFF_W = DIFF_HEADS * 2 * DIFF_D
INPROJ_SUBTILES = 2
MERGE_SUBTILES = 2
DIFF_Q_TILE = 2048
DIFF_SUB_ROWS = 128

PEER_HEADS = 8
PEER_NKEYS = 128
PEER_EXPERTS = PEER_NKEYS * PEER_NKEYS
PEER_HALF = 128
PEER_TOPK = 16

COL_RQ = 0
COL_RK = COL_RQ + RET_QK_W
COL_RV = COL_RK + RET_QK_W
COL_RG = COL_RV + RET_V_W
COL_DQ = COL_RG + RET_V_W
COL_DK = COL_DQ + DIFF_W
COL_DV = COL_DK + DIFF_W
COL_GR = COL_DV + DIFF_W
COL_GD = COL_GR + D_MODEL
IN_COLS = COL_GD + D_MODEL

LANES = 128
BF16_ROWS = 16
EXPERT_TOKENS = 1024
EXPERT_LANE_CHUNK = 512
EXPERT_TILE_ROWS = 16
EXPERT_ROW_CHUNKS = 8
VMEM_LIMIT = 56 << 20
EXPERT_VMEM_LIMIT = 58 << 20

LOG2_E = math.log2(math.e)

NT_DIMS = (((1,), (1,)), ((), ()))
TN_DIMS = (((0,), (0,)), ((), ()))


def _cparams(*sem, vmem_limit=VMEM_LIMIT):
    return pltpu.CompilerParams(dimension_semantics=sem, vmem_limit_bytes=vmem_limit)


def _resident(shape):
    nd = len(shape)
    return pl.BlockSpec(shape, lambda *_: (0,) * nd, pipeline_mode=pl.Buffered(1))


def _mod_kernel(c_ref, w_ref, b_ref, o_ref):
    c = c_ref[...]
    s = c * jax.nn.sigmoid(c)
    o_ref[...] = jnp.dot(s, w_ref[...], preferred_element_type=F32,
                         precision=lax.Precision.HIGHEST) + b_ref[...]


def _modulation(cc, w, b):
    rows, d = cc.shape
    n = w.shape[1]
    tn = 768
    return pl.pallas_call(
        _mod_kernel,
        grid=(n // tn,),
        in_specs=[pl.BlockSpec((rows, d), lambda j: (0, 0)),
                  pl.BlockSpec((d, tn), lambda j: (0, j)),
                  pl.BlockSpec((1, tn), lambda j: (0, j))],
        out_specs=pl.BlockSpec((rows, tn), lambda j: (0, j)),
        out_shape=jax.ShapeDtypeStruct((rows, n), F32),
        compiler_params=_cparams("arbitrary"),
        name="modulation",
    )(cc, w, b.reshape(1, n))


def _rope_tables(seq, head_dim):
    rows = seq // GRID_W
    row = jnp.repeat(jnp.arange(rows, dtype=F32), GRID_W)
    col = jnp.tile(jnp.arange(GRID_W, dtype=F32), rows)
    half = head_dim // 2
    pair = half // 2
    lane = jnp.arange(LANES)
    d = lane % head_dim
    inv = ROPE_BASE ** (-jnp.arange(pair, dtype=F32) / pair)
    freq = inv[d % pair]
    pos = jnp.where((d < half)[None, :], row[:, None], col[:, None])
    ang = pos * freq[None, :]
    cos = jnp.cos(ang)
    sin = jnp.sin(ang)
    first = ((d % half) < pair)[None, :]
    sin_a = jnp.where(first, -sin, 0.0)
    sin_b = jnp.where(first, 0.0, sin)
    return cos, sin_a, sin_b, pair


def _rope(acc, cos, sin_a, sin_b, pair):
    up = pltpu.roll(acc, LANES - pair, 1)
    dn = pltpu.roll(acc, pair, 1)
    return acc * cos + up * sin_a + dn * sin_b


def _inproj_kernel(*refs, specs, rope, ret_pair, diff_pair):
    if rope:
        (x_ref, g_ref, sc_ref, sh_ref, w_ref,
         rc_ref, ra_ref, rb_ref, dc_ref, da_ref, db_ref) = refs[:11]
        out_refs = refs[11:]
    else:
        x_ref, g_ref, sc_ref, sh_ref, w_ref = refs[:5]
        out_refs = refs[5:]
    sub = x_ref.shape[0] // INPROJ_SUBTILES
    for t in range(INPROJ_SUBTILES):
        rs = slice(t * sub, (t + 1) * sub)
        x = x_ref[rs, :]
        y = x * lax.rsqrt(jnp.mean(x * x, axis=-1, keepdims=True) + EPS)
        u = (y * g_ref[...]) * (1.0 + sc_ref[0]) + sh_ref[0]
        ub = u.astype(BF16)
        for (col0, width, kind), o_ref in zip(specs, out_refs):
            for c in range(0, width, 512):
                cw = min(512, width - c)
                acc = jnp.dot(ub, w_ref[:, col0 + c:col0 + c + cw], preferred_element_type=F32)
                if kind in ("ret_q", "ret_k", "diff_q", "diff_k"):
                    for l in range(0, cw, LANES):
                        a = acc[:, l:l + LANES]
                        if kind.startswith("ret"):
                            r = _rope(a, rc_ref[rs, :], ra_ref[rs, :], rb_ref[rs, :], ret_pair)
                        else:
                            r = _rope(a, dc_ref[rs, :], da_ref[rs, :], db_ref[rs, :], diff_pair)
                        if kind == "ret_q":
                            r = r * (RET_DK ** -0.5)
                        elif kind == "diff_q":
                            r = r * (DIFF_D ** -0.5 * LOG2_E)
                        o_ref[rs, c + l:c + l + LANES] = r.astype(BF16)
                elif kind == "silu":
                    o_ref[rs, c:c + cw] = (acc * jax.nn.sigmoid(acc)).astype(BF16)
                elif kind == "sigmoid":
                    o_ref[rs, c:c + cw] = jax.nn.sigmoid(acc).astype(BF16)
                else:
                    o_ref[rs, c:c + cw] = acc.astype(BF16)


def _inproj(x2, gain, scale, shift, w_bf, specs, seq, tables):
    n, d = x2.shape
    tm = min(512, seq)
    assert seq % tm == 0 and n % seq == 0
    per_seq = seq // tm
    nb = scale.shape[0]
    if nb == 1:
        mod_map = lambda i: (0, 0, 0)
    else:
        mod_map = lambda i: (i // per_seq, 0, 0)
    rope = tables is not None
    in_specs = [pl.BlockSpec((tm, d), lambda i: (i, 0)),
                pl.BlockSpec((1, d), lambda i: (0, 0)),
                pl.BlockSpec((1, 1, d), mod_map),
                pl.BlockSpec((1, 1, d), mod_map),
                _resident(w_bf.shape)]
    args = [x2, gain.reshape(1, d), scale, shift, w_bf]
    ret_pair = diff_pair = 0
    if rope:
        (rc, ra, rb, ret_pair), (dc, da, db, diff_pair) = tables
        tab_spec = pl.BlockSpec((tm, LANES), lambda i: (i % per_seq, 0))
        in_specs += [tab_spec] * 6
        args += [rc, ra, rb, dc, da, db]
    out_specs = [pl.BlockSpec((tm, w), lambda i: (i, 0)) for (_, w, _) in specs]
    out_shape = [jax.ShapeDtypeStruct((n, w), BF16) for (_, w, _) in specs]
    kern = functools.partial(_inproj_kernel, specs=tuple(specs), rope=rope,
                             ret_pair=ret_pair, diff_pair=diff_pair)
    return pl.pallas_call(
        kern, grid=(n // tm,), in_specs=in_specs, out_specs=out_specs, out_shape=out_shape,
        compiler_params=_cparams("parallel"),
        name="inproj_rope" if rope else "inproj_ctx",
    )(*args)


def _ret_kernel(sdec_ref, q_ref, k_ref, v_ref, g_ref, kc_ref, vc_ref, m_ref, dec_ref, cdec_ref,
                gn_ref, o_ref, ybuf, kv, st, sf, sb, *, n_chunks):
    C = RET_CHUNK
    DK = RET_DK
    h = pl.program_id(1)
    sdec_f = sdec_ref[2 * h]
    sdec_b = sdec_ref[2 * h + 1]

    kc = kc_ref[...].astype(F32)
    vc = vc_ref[...]
    sf[...] = lax.dot_general((kc * cdec_ref[0, 0]).astype(BF16), vc, TN_DIMS,
                              preferred_element_type=F32)
    sb[...] = lax.dot_general((kc * cdec_ref[0, 1]).astype(BF16), vc, TN_DIMS,
                              preferred_element_type=F32)

    dmat = m_ref[0]
    qdec_f = dec_ref[0, 0]
    kdec_f = dec_ref[0, 1]
    qdec_b = dec_ref[0, 2]
    kdec_b = dec_ref[0, 3]
    gn = gn_ref[...]

    def intra(i, carry):
        r = pl.multiple_of(i * C, C)
        q = q_ref[pl.ds(r, C), :]
        k = k_ref[pl.ds(r, C), :]
        v = v_ref[pl.ds(r, C), :]
        kf = k.astype(F32)
        att = lax.dot_general(q, k, NT_DIMS, preferred_element_type=F32) * dmat
        ybuf[pl.ds(r, C), :] = jnp.dot(att.astype(BF16), v, preferred_element_type=F32)
        kk = jnp.concatenate([(kf * kdec_f).astype(BF16), (kf * kdec_b).astype(BF16)], axis=1)
        kv[i] = lax.dot_general(kk, v, TN_DIMS, preferred_element_type=F32)
        return carry

    lax.fori_loop(0, n_chunks, intra, 0, unroll=min(RET_UNROLL, n_chunks))

    def scan(t, carry):
        i = t
        st[i, :DK, :] = sf[...].astype(BF16)
        sf[...] = sdec_f * sf[...] + kv[i, :DK, :]
        j = n_chunks - 1 - t
        st[j, DK:, :] = sb[...].astype(BF16)
        sb[...] = sdec_b * sb[...] + kv[j, DK:, :]
        return carry

    lax.fori_loop(0, n_chunks, scan, 0)

    def cross(i, carry):
        r = pl.multiple_of(i * C, C)
        qf = q_ref[pl.ds(r, C), :].astype(F32)
        qq = jnp.concatenate([(qf * qdec_f).astype(BF16), (qf * qdec_b).astype(BF16)], axis=1)
        y = ybuf[pl.ds(r, C), :] + jnp.dot(qq, st[i], preferred_element_type=F32)
        mu = jnp.mean(y, axis=-1, keepdims=True)
        yc = y - mu
        var = jnp.mean(yc * yc, axis=-1, keepdims=True)
        yn = yc * lax.rsqrt(var + EPS) * gn
        o_ref[pl.ds(r, C), :] = (g_ref[pl.ds(r, C), :].astype(F32) * yn).astype(BF16)
        return carry

    lax.fori_loop(0, n_chunks, cross, 0, unroll=min(RET_UNROLL, n_chunks))


def _retention(ret_q, ret_k, ret_v, ret_g, ctx_rk, ctx_rv, dec_f, dec_b, gn_g, batch, seq, ctx_len):
    C = RET_CHUNK
    H = RET_HEADS
    lg_f = jnp.log1p(-jnp.exp2(dec_f.astype(F32)))
    lg_b = jnp.log1p(-jnp.exp2(dec_b.astype(F32)))
    idx = jnp.arange(C, dtype=F32)
    dist = idx[:, None] - idx[None, :]
    dmat = jnp.where(dist[None] >= 0,
                     jnp.exp(lg_f[:, None, None] * jnp.maximum(dist, 0.0)[None]),
                     jnp.exp(lg_b[:, None, None] * jnp.maximum(-dist, 0.0)[None]))
    qdec_f = jnp.exp(lg_f[:, None] * (idx + 1.0))
    kdec_f = jnp.exp(lg_f[:, None] * (C - 1.0 - idx))
    qdec_b = jnp.exp(lg_b[:, None] * (C - idx))
    kdec_b = jnp.exp(lg_b[:, None] * idx)
    dec = jnp.stack([qdec_f, kdec_f, qdec_b, kdec_b], axis=1)
    dec = jnp.broadcast_to(dec[..., None], (H, 4, C, RET_DK))
    cidx = jnp.arange(ctx_len, dtype=F32)
    cdec = jnp.stack([jnp.exp(lg_f[:, None] * (ctx_len - 1.0 - cidx)),
                      jnp.exp(lg_b[:, None] * cidx)], axis=1)
    cdec = jnp.broadcast_to(cdec[..., None], (H, 2, ctx_len, RET_DK))
    sdec = jnp.stack([jnp.exp(lg_f * C), jnp.exp(lg_b * C)], axis=1).reshape(2 * H)

    n_chunks = seq // C
    kern = functools.partial(_ret_kernel, n_chunks=n_chunks)
    return pl.pallas_call(
        kern,
        grid=(batch, H),
        in_specs=[pl.BlockSpec(memory_space=pltpu.SMEM),
                  pl.BlockSpec((seq, RET_DK), lambda b, h: (b, h)),
                  pl.BlockSpec((seq, RET_DK), lambda b, h: (b, h)),
                  pl.BlockSpec((seq, RET_DV), lambda b, h: (b, h)),
                  pl.BlockSpec((seq, RET_DV), lambda b, h: (b, h)),
                  pl.BlockSpec((ctx_len, RET_DK), lambda b, h: (b, h)),
                  pl.BlockSpec((ctx_len, RET_DV), lambda b, h: (b, h)),
                  pl.BlockSpec((1, C, C), lambda b, h: (h, 0, 0)),
                  pl.BlockSpec((1, 4, C, RET_DK), lambda b, h: (h, 0, 0, 0)),
                  pl.BlockSpec((1, 2, ctx_len, RET_DK), lambda b, h: (h, 0, 0, 0)),
                  pl.BlockSpec((1, RET_DV), lambda b, h: (0, h))],
        out_specs=pl.BlockSpec((seq, RET_DV), lambda b, h: (b, h)),
        out_shape=jax.ShapeDtypeStruct((batch * seq, RET_V_W), BF16),
        scratch_shapes=[pltpu.VMEM((seq, RET_DV), F32),
                        pltpu.VMEM((n_chunks, 2 * RET_DK, RET_DV), F32),
                        pltpu.VMEM((n_chunks, 2 * RET_DK, RET_DV), BF16),
                        pltpu.VMEM((RET_DK, RET_DV), F32),
                        pltpu.VMEM((RET_DK, RET_DV), F32)],
        compiler_params=_cparams("parallel", "arbitrary"),
        name="retention",
    )(sdec, ret_q, ret_k, ret_v, ret_g, ctx_rk, ctx_rv, dmat, dec, cdec,
      gn_g.reshape(1, RET_V_W))


def _diff_kernel(q_ref, kl_ref, kc_ref, vl_ref, vc_ref, lp_ref, gn_ref, o_ref, *, lam_init):
    kl = kl_ref[...]
    kc = kc_ref[...]
    tq, hw = q_ref.shape
    vl = jnp.concatenate([vl_ref[...], jnp.ones(vl_ref.shape, BF16)], axis=1)
    vc = jnp.concatenate([vc_ref[...], jnp.ones(vc_ref.shape, BF16)], axis=1)
    lp = lp_ref[...]
    lam = (jnp.exp(jnp.sum(lp[0:1] * lp[1:2], axis=-1, keepdims=True))
           - jnp.exp(jnp.sum(lp[2:3] * lp[3:4], axis=-1, keepdims=True)) + lam_init)

    def branch(qm):
        sl = lax.dot_general(qm, kl, NT_DIMS, preferred_element_type=F32)
        sc = lax.dot_general(qm, kc, NT_DIMS, preferred_element_type=F32)
        m = jnp.maximum(jnp.max(sl, axis=-1, keepdims=True), jnp.max(sc, axis=-1, keepdims=True))
        pl_ = jnp.exp2(sl - m)
        pc = jnp.exp2(sc - m)
        o = (jnp.dot(pl_.astype(BF16), vl, preferred_element_type=F32)
             + jnp.dot(pc.astype(BF16), vc, preferred_element_type=F32))
        return o[:, :hw] / o[:, hw:]

    sub = DIFF_SUB_ROWS
    for t in range(tq // sub):
        rows = slice(t * sub, (t + 1) * sub)
        q = q_ref[rows, :]
        lane = lax.broadcasted_iota(jnp.int32, q.shape, 1)
        zero = jnp.zeros_like(q)
        o1 = branch(jnp.where(lane < DIFF_D, q, zero))
        o2 = branch(jnp.where(lane >= DIFF_D, q, zero))
        o = o1 - lam * o2
        on = o * lax.rsqrt(jnp.mean(o * o, axis=-1, keepdims=True) + EPS) * (1.0 - lam_init)
        o_ref[rows, :] = (on * gn_ref[...]).astype(BF16)


def _diff_attention(dq, dk, dv, ctx_dk, ctx_dv, lam_params, gn_g, lam_init, batch, seq, ctx_len):
    tq = min(DIFF_Q_TILE, seq)
    nq = seq // tq
    hw = 2 * DIFF_D
    kern = functools.partial(_diff_kernel, lam_init=lam_init)
    return pl.pallas_call(
        kern,
        grid=(batch, DIFF_HEADS, nq),
        in_specs=[pl.BlockSpec((tq, hw), lambda b, h, i: (b * nq + i, h)),
                  pl.BlockSpec((seq, hw), lambda b, h, i: (b, h)),
                  pl.BlockSpec((ctx_len, hw), lambda b, h, i: (b, h)),
                  pl.BlockSpec((seq, hw), lambda b, h, i: (b, h)),
                  pl.BlockSpec((ctx_len, hw), lambda b, h, i: (b, h)),
                  pl.BlockSpec((4, DIFF_D), lambda b, h, i: (0, 0)),
                  pl.BlockSpec((1, hw), lambda b, h, i: (0, h))],
        out_specs=pl.BlockSpec((tq, hw), lambda b, h, i: (b * nq + i, h)),
        out_shape=jax.ShapeDtypeStruct((batch * seq, DIFF_W), BF16),
        compiler_params=_cparams("parallel", "parallel", "arbitrary"),
        name="diff_attention",
    )(dq, dk, ctx_dk, dv, ctx_dv, lam_params, gn_g.reshape(1, DIFF_W))


def _merge_kernel(x_ref, zr_ref, zd_ref, gr_ref, gd_ref, wr_ref, wd_ref, wo_ref,
                  pmg_ref, pfg_ref, ga_ref, shf_ref, scf_ref, x1_ref, f_ref):
    sub = x_ref.shape[0] // MERGE_SUBTILES
    for t in range(MERGE_SUBTILES):
        rows = slice(t * sub, (t + 1) * sub)
        p_ret = jnp.dot(zr_ref[rows, :], wr_ref[...], preferred_element_type=F32)
        p_diff = jnp.dot(zd_ref[rows, :], wd_ref[...], preferred_element_type=F32)
        m = gr_ref[rows, :].astype(F32) * p_ret + gd_ref[rows, :].astype(F32) * p_diff
        mix = jnp.dot(m.astype(BF16), wo_ref[...], preferred_element_type=F32)
        mixn = mix * lax.rsqrt(jnp.mean(mix * mix, axis=-1, keepdims=True) + EPS) * pmg_ref[...]
        x1 = x_ref[rows, :] + ga_ref[0] * mixn
        x1_ref[rows, :] = x1
        fn = x1 * lax.rsqrt(jnp.mean(x1 * x1, axis=-1, keepdims=True) + EPS) * pfg_ref[...]
        f_ref[rows, :] = (fn * (1.0 + scf_ref[0]) + shf_ref[0]).astype(BF16)


def _merge(x2, z_ret, z_diff, g_r, g_d, w_br_ret, w_br_diff, w_out, post_mix_g, pre_ffn_g,
           ga_a, sh_f, sc_f, seq):
    n, d = x2.shape
    tm = min(512, seq)
    per_seq = seq // tm
    row = lambda i: (i, 0)
    mod = lambda i: (i // per_seq, 0, 0)
    vec = pl.BlockSpec((1, d), lambda i: (0, 0))
    return pl.pallas_call(
        _merge_kernel,
        grid=(n // tm,),
        in_specs=[pl.BlockSpec((tm, d), row)] * 5
                 + [_resident((d, d))] * 3
                 + [vec, vec]
                 + [pl.BlockSpec((1, 1, d), mod)] * 3,
        out_specs=[pl.BlockSpec((tm, d), row), pl.BlockSpec((tm, d), row)],
        out_shape=[jax.ShapeDtypeStruct((n, d), F32), jax.ShapeDtypeStruct((n, d), BF16)],
        compiler_params=_cparams("parallel"),
        name="merge",
    )(x2, z_ret, z_diff, g_r, g_d, w_br_ret.astype(BF16), w_br_diff.astype(BF16),
      w_out.astype(BF16), post_mix_g.reshape(1, d), pre_ffn_g.reshape(1, d), ga_a, sh_f, sc_f)


def _staircase():
    return [(p, q) for p in range(PEER_TOPK) for q in range(PEER_TOPK)
            if (p + 1) * (q + 1) <= PEER_TOPK]


def _bf16_pair_word(x):
    hi = pltpu.bitcast(x.astype(BF16).astype(F32), jnp.uint32)
    return hi | (hi >> 16)


def _bf16_rows(word_row):
    w = jnp.broadcast_to(word_row, (BF16_ROWS // 2, word_row.shape[1]))
    return pltpu.bitcast(w, BF16)


def _count_leading(rows, test, like):
    assert len(rows) == 16
    count = jnp.zeros_like(like)
    decisions = []
    for width in (8, 4, 2, 1):
        cands = [rows[base + width - 1] for base in range(0, 16, 2 * width)]
        for bit in reversed(decisions):
            cands = [jnp.where(bit, hi, lo) for lo, hi in zip(cands[0::2], cands[1::2])]
        passed = test(cands[0])
        decisions.append(passed)
        count = count + jnp.where(passed, float(width), 0.0)
    return jnp.where(test(rows[15]), 16.0, count)


def _sort16_pairs():
    pairs = []

    def merge(lo, hi, r):
        step = r * 2
        if step < hi - lo:
            merge(lo, hi, step)
            merge(lo + r, hi, step)
            pairs.extend((i, i + r) for i in range(lo + r, hi - r, step))
        else:
            pairs.append((lo, lo + r))

    def sort(lo, hi):
        if hi - lo >= 1:
            mid = lo + (hi - lo) // 2
            sort(lo, mid)
            sort(mid + 1, hi)
            merge(lo, hi, 1)

    sort(0, 15)
    return pairs


def _cx(a, b):
    if b is None:
        return a, None
    if a is None:
        return b, None
    return jnp.maximum(a, b), jnp.minimum(a, b)


def _sort16(vals):
    cur = list(vals)
    for i, j in _sort16_pairs():
        cur[i], cur[j] = _cx(cur[i], cur[j])
    return cur


def _merge_top16(a, b):
    cur = [_cx(a[r], b[15 - r])[0] for r in range(16)]
    for stride in (8, 4, 2, 1):
        for i in range(16):
            if i & stride == 0:
                cur[i], cur[i + stride] = _cx(cur[i], cur[i + stride])
    return cur


def _top16_sorted(slabs):
    cur = _sort16(slabs)
    for shift in (4, 2, 1):
        cur = _merge_top16(cur, [pltpu.roll(x, shift, 0) for x in cur])
    return cur


def _route_kernel(fb_ref, wq_ref, sk_ref, r2_ref, n1_ref, e1_ref, e2_ref, s_scr, top):
    K = PEER_NKEYS
    G = 2 * PEER_HEADS
    H = PEER_HEADS
    tt = fb_ref.shape[0]
    qt = lax.dot_general(wq_ref[...], fb_ref[...], NT_DIMS, preferred_element_type=F32)
    for g in range(G):
        s_scr[g * K:(g + 1) * K, :] = jnp.dot(sk_ref[g], qt[g * K:(g + 1) * K, :].astype(BF16),
                                              preferred_element_type=F32)

    for g in range(G):
        hh, a = divmod(g, 2)
        for c in range(0, tt, LANES):
            lanes = slice(c, c + LANES)
            best = _top16_sorted([s_scr[g * K + r * 8:g * K + (r + 1) * 8, lanes]
                                  for r in range(PEER_TOPK)])
            for p in range(PEER_TOPK):
                row = (a * PEER_TOPK + p) * H + hh
                top[row:row + 1, lanes] = best[p][0:1, :]

    tops_a = [top[p * H:(p + 1) * H, :] for p in range(PEER_TOPK)]
    tops_b = [top[(PEER_TOPK + q) * H:(PEER_TOPK + q + 1) * H, :] for q in range(PEER_TOPK)]
    pairs = _staircase()
    cand = [tops_a[p] + tops_b[q] for (p, q) in pairs]
    assert [pq[0] for pq in pairs[:PEER_TOPK]] == [0] * PEER_TOPK
    padded = cand + [None] * (-len(cand) % PEER_TOPK)
    lists = [padded[:PEER_TOPK]] + [_sort16(padded[i:i + PEER_TOPK])
                                    for i in range(PEER_TOPK, len(padded), PEER_TOPK)]
    while len(lists) > 1:
        lists = [_merge_top16(lists[i], lists[i + 1]) if i + 1 < len(lists) else lists[i]
                 for i in range(0, len(lists), 2)]
    tau = lists[0][PEER_TOPK - 1]
    c00 = cand[0]
    z = functools.reduce(
        lambda a, b: a + b,
        [jnp.where(c >= tau, jnp.exp(c - c00), 0.0) for c in cand])
    zinv = 1.0 / z

    for hh in range(H):
        s1 = s_scr[(2 * hh) * K:(2 * hh + 1) * K, :]
        s2 = s_scr[(2 * hh + 1) * K:(2 * hh + 2) * K, :]
        tau_h = tau[hh:hh + 1, :]
        desc = [tops_b[q][hh:hh + 1, :] for q in range(PEER_TOPK)]
        cnt = _count_leading(desc, lambda b: s1 + b >= tau_h, s1)
        n1_ref[hh] = _bf16_pair_word(cnt)
        e1_ref[hh] = _bf16_pair_word(jnp.exp(s1 - tops_a[0][hh:hh + 1, :]) * zinv[hh:hh + 1, :])
        e2 = jnp.exp(s2 - tops_b[0][hh:hh + 1, :]).astype(BF16)
        e2_ref[hh] = pltpu.bitcast(e2, jnp.uint32)
        asc = [tops_b[PEER_TOPK - 1 - q][hh:hh + 1, :] for q in range(PEER_TOPK)]
        rank = float(PEER_TOPK) - _count_leading(asc, lambda b: s2 >= b, s2)
        r2_ref[hh] = pltpu.bitcast(rank.astype(BF16), jnp.uint32)


def _route(f, w_q, sub_keys, seq):
    n, d = f.shape
    tt = min(256, seq)
    H, K = PEER_HEADS, PEER_NKEYS
    wq_t = w_q.T.astype(BF16)
    sk = sub_keys.reshape(2 * H, K, PEER_HALF).astype(BF16)
    row_tab = jax.ShapeDtypeStruct((H, K, n), jnp.uint32)
    key_tab = jax.ShapeDtypeStruct((H, K // 2, n), jnp.uint32)
    row_spec = pl.BlockSpec((H, K, tt), lambda t: (0, 0, t))
    key_spec = pl.BlockSpec((H, K // 2, tt), lambda t: (0, 0, t))
    return pl.pallas_call(
        _route_kernel,
        grid=(n // tt,),
        in_specs=[pl.BlockSpec((tt, d), lambda t: (t, 0)),
                  _resident(wq_t.shape), _resident(sk.shape)],
        out_specs=[key_spec, row_spec, row_spec, key_spec],
        out_shape=[key_tab, row_tab, row_tab, key_tab],
        scratch_shapes=[pltpu.VMEM((2 * H * K, tt), F32),
                        pltpu.VMEM((2 * PEER_TOPK * H, tt), F32)],
        compiler_params=_cparams("parallel"),
        name="peer_route",
    )(f, wq_t, sk)


def _gelu(x):
    return 0.5 * x * (1.0 + lax.erf(x * (2.0 ** -0.5)))


def _expert_kernel(fb_ref, u_ref, vt_ref, r2_ref, e2_ref, n1_ref, e1_ref, x1_ref, g_ref, ga_ref,
                   o_ref, acc, h_scr, w_scr, *, rows_per_tile):
    e = pl.program_id(1)
    K = PEER_NKEYS

    @pl.when(e == 0)
    def _():
        acc[...] = jnp.zeros_like(acc)

    tt = fb_ref.shape[0]
    groups = K // BF16_ROWS
    lane_chunk = min(tt, EXPERT_LANE_CHUNK)
    zero = jnp.zeros((BF16_ROWS, lane_chunk), BF16)
    fb = fb_ref[...]
    per_chunk = rows_per_tile // EXPERT_ROW_CHUNKS
    for i in range(rows_per_tile):
        if i % per_chunk == 0:
            chunk = slice(i * K, (i + per_chunk) * K)
            h_scr[chunk, :] = lax.dot_general(u_ref[chunk, :], fb, NT_DIMS,
                                              preferred_element_type=F32).astype(BF16)
        for c in range(0, tt, lane_chunk):
            lanes = slice(c, c + lane_chunk)
            gate = [None] * groups
            for hh in range(PEER_HEADS):
                n1 = _bf16_rows(n1_ref[hh, i:i + 1, lanes])
                e1 = _bf16_rows(e1_ref[hh, i:i + 1, lanes])
                for r in range(groups):
                    words = slice(r * BF16_ROWS // 2, (r + 1) * BF16_ROWS // 2)
                    r2 = pltpu.bitcast(r2_ref[hh, words, lanes], BF16)
                    e2 = pltpu.bitcast(e2_ref[hh, words, lanes], BF16)
                    term = jnp.where(r2 < n1, e1 * e2, zero)
                    gate[r] = term if hh == 0 else gate[r] + term
            for r in range(groups):
                rows = slice(i * K + r * BF16_ROWS, i * K + (r + 1) * BF16_ROWS)
                w_scr[rows, lanes] = gate[r] * _gelu(h_scr[rows, lanes])
    acc[...] += jnp.dot(vt_ref[...], w_scr[...], preferred_element_type=F32)

    @pl.when(e == pl.num_programs(1) - 1)
    def _():
        y = acc[...]
        yn = y * lax.rsqrt(jnp.mean(y * y, axis=0, keepdims=True) + EPS)
        o_ref[...] = x1_ref[...] + ga_ref[0] * (yn.T * g_ref[...])


def _experts(fb, u_bf, vt_bf, r2, n1, e1, e2, x1, post_ffn_g, ga_f, seq):
    n, d = fb.shape
    tt = min(EXPERT_TOKENS, seq)
    per_seq = seq // tt
    rows_per_tile = EXPERT_TILE_ROWS
    et = rows_per_tile * PEER_NKEYS
    H, K = PEER_HEADS, PEER_NKEYS
    kern = functools.partial(_expert_kernel, rows_per_tile=rows_per_tile)
    token_tile = pl.BlockSpec((tt, d), lambda t, e: (t, 0), pipeline_mode=pl.Buffered(1))
    return pl.pallas_call(
        kern,
        grid=(n // tt, PEER_EXPERTS // et),
        in_specs=[token_tile,
                  pl.BlockSpec((et, d), lambda t, e: (e, 0)),
                  pl.BlockSpec((d, et), lambda t, e: (0, e)),
                  pl.BlockSpec((H, K // 2, tt), lambda t, e: (0, 0, t)),
                  pl.BlockSpec((H, K // 2, tt), lambda t, e: (0, 0, t)),
                  pl.BlockSpec((H, rows_per_tile, tt), lambda t, e: (0, e, t)),
                  pl.BlockSpec((H, rows_per_tile, tt), lambda t, e: (0, e, t)),
                  token_tile,
                  pl.BlockSpec((1, d), lambda t, e: (0, 0)),
                  pl.BlockSpec((1, 1, d), lambda t, e: (t // per_seq, 0, 0))],
        out_specs=pl.BlockSpec((tt, d), lambda t, e: (t, 0)),
        out_shape=jax.ShapeDtypeStruct((n, d), F32),
        scratch_shapes=[pltpu.VMEM((d, tt), F32), pltpu.VMEM((et, tt), BF16),
                        pltpu.VMEM((et, tt), BF16)],
        compiler_params=_cparams("parallel", "arbitrary", vmem_limit=EXPERT_VMEM_LIMIT),
        name="peer_experts",
    )(fb, u_bf, vt_bf, r2, e2, n1, e1, x1, post_ffn_g.reshape(1, d), ga_f)


def kernel(x, c, ctx, c_ctx, w_mod, b_mod, pre_mix_g, post_mix_g, pre_ffn_g, post_ffn_g, w_in,
           ret_decay_fwd, ret_decay_bwd, ret_gn_g, diff_lambda, diff_gn_g, w_br_ret, w_br_diff,
           w_out, peer_w_q, peer_sub_keys, peer_u, peer_v):
    batch, seq, d = x.shape
    ctx_len = ctx.shape[1]
    depth = w_mod.shape[0]
    assert depth == 1 and d == D_MODEL
    l = 0
    lam_init = 0.8 - 0.6 * math.exp(-0.3 * l)

    rows = ((batch + 1 + 7) // 8) * 8
    cc = jnp.zeros((rows, d), F32).at[:batch].set(c).at[batch].set(c_ctx)
    mod = _modulation(cc, w_mod[l], b_mod[l])
    sh_a, sc_a, ga_a, sh_f, sc_f, ga_f = [t[:batch, None, :] for t in jnp.split(mod, 6, axis=-1)]
    csh_a, csc_a = [t[batch:batch + 1, None, :] for t in jnp.split(mod, 6, axis=-1)[:2]]

    w_in_bf = w_in[l].astype(BF16)
    x2 = x.reshape(batch * seq, d)
    ctx2 = ctx.reshape(batch * ctx_len, d)

    lat_specs = [(COL_RQ, RET_QK_W, "ret_q"), (COL_RK, RET_QK_W, "ret_k"),
                 (COL_RV, RET_V_W, "plain"), (COL_RG, RET_V_W, "silu"),
                 (COL_DQ, DIFF_W, "diff_q"), (COL_DK, DIFF_W, "diff_k"),
                 (COL_DV, DIFF_W, "plain"), (COL_GR, D_MODEL, "sigmoid"),
                 (COL_GD, D_MODEL, "sigmoid")]
    tables = (_rope_tables(seq, RET_DK), _rope_tables(seq, DIFF_D))
    rq, rk, rv, rg, dq, dk, dv, g_r, g_d = _inproj(
        x2, pre_mix_g[l], sc_a, sh_a, w_in_bf, lat_specs, seq, tables)

    ctx_specs = [(COL_RK, RET_QK_W, "plain"), (COL_RV, RET_V_W, "plain"),
                 (COL_DK, DIFF_W, "plain"), (COL_DV, DIFF_W, "plain")]
    rk_c, rv_c, dk_c, dv_c = _inproj(ctx2, pre_mix_g[l], csc_a, csh_a, w_in_bf, ctx_specs,
                                     ctx_len, None)

    z_ret = _retention(rq, rk, rv, rg, rk_c, rv_c, ret_decay_fwd[l], ret_decay_bwd[l],
                       ret_gn_g[l], batch, seq, ctx_len)
    z_diff = _diff_attention(dq, dk, dv, dk_c, dv_c, diff_lambda[l], diff_gn_g[l], lam_init,
                             batch, seq, ctx_len)
    x1, fb = _merge(x2, z_ret, z_diff, g_r, g_d, w_br_ret[l], w_br_diff[l], w_out[l],
                   post_mix_g[l], pre_ffn_g[l], ga_a, sh_f, sc_f, seq)

    r2, n1, e1, e2 = _route(fb, peer_w_q[l], peer_sub_keys[l], seq)
    out = _experts(fb, peer_u[l].astype(BF16), peer_v[l].T.astype(BF16), r2, n1, e1, e2, x1,
                   post_ffn_g[l], ga_f, seq)
    return out.reshape(batch, seq, d)
```

```python
import functools
import math

import jax
import jax.numpy as jnp
from jax import lax
from jax.experimental import pallas as pl
from jax.experimental.pallas import tpu as pltpu

F32 = jnp.float32
BF16 = jnp.bfloat16

D_MODEL = 1024
GRID_W = 64
EPS = 1e-6
ROPE_BASE = 10000.0

RET_HEADS = 4
RET_DK = 128
RET_DV = 256
RET_CHUNK = 128
RET_QK_W = RET_HEADS * RET_DK
RET_V_W = RET_HEADS * RET_DV

DIFF_HEADS = 8
DIFF_D = 64
DIFF_W = DIFF_HEADS * 2 * DIFF_D

PEER_HEADS = 8
PEER_NKEYS = 128
PEER_EXPERTS = PEER_NKEYS * PEER_NKEYS
PEER_HALF = 128
PEER_TOPK = 16

COL_RQ = 0
COL_RK = COL_RQ + RET_QK_W
COL_RV = COL_RK + RET_QK_W
COL_RG = COL_RV + RET_V_W
COL_DQ = COL_RG + RET_V_W
COL_DK = COL_DQ + DIFF_W
COL_DV = COL_DK + DIFF_W
COL_GR = COL_DV + DIFF_W
COL_GD = COL_GR + D_MODEL
IN_COLS = COL_GD + D_MODEL

LANES = 128
SUBLANES = 8
BF16_ROWS = 2 * SUBLANES
V7X_VMEM_BYTES = 64 << 20

MOD_COLS = 768
ROW_TILE = 512
INPROJ_SUBTILES = 2
INPROJ_COLS = 512
MERGE_SUBTILES = 2
RET_UNROLL = 16
DIFF_Q_TILE = 2048
DIFF_SUB_ROWS = 128
ROUTE_TOKENS = 256
EXPERT_TOKENS = 1024
EXPERT_LANE_CHUNK = 512
EXPERT_TILE_ROWS = 16
EXPERT_ROW_CHUNKS = 8
VMEM_LIMIT = V7X_VMEM_BYTES - (8 << 20)
EXPERT_VMEM_LIMIT = V7X_VMEM_BYTES - (6 << 20)

LOG2_E = math.log2(math.e)

NT_DIMS = (((1,), (1,)), ((), ()))
TN_DIMS = (((0,), (0,)), ((), ()))


def _cparams(*sem, vmem_limit=VMEM_LIMIT):
    return pltpu.CompilerParams(dimension_semantics=sem, vmem_limit_bytes=vmem_limit)


def _resident(shape):
    nd = len(shape)
    return pl.BlockSpec(shape, lambda *_: (0,) * nd, pipeline_mode=pl.Buffered(1))


def _mod_kernel(c_ref, w_ref, b_ref, o_ref):
    c = c_ref[...]
    s = c * jax.nn.sigmoid(c)
    o_ref[...] = jnp.dot(s, w_ref[...], preferred_element_type=F32,
                         precision=lax.Precision.HIGHEST) + b_ref[...]


def _modulation(cc, w, b):
    rows, d = cc.shape
    n = w.shape[1]
    tn = MOD_COLS
    return pl.pallas_call(
        _mod_kernel,
        grid=(n // tn,),
        in_specs=[pl.BlockSpec((rows, d), lambda j: (0, 0)),
                  pl.BlockSpec((d, tn), lambda j: (0, j)),
                  pl.BlockSpec((1, tn), lambda j: (0, j))],
        out_specs=pl.BlockSpec((rows, tn), lambda j: (0, j)),
        out_shape=jax.ShapeDtypeStruct((rows, n), F32),
        compiler_params=_cparams("arbitrary"),
        name="modulation",
    )(cc, w, b.reshape(1, n))


def _rope_tables(seq, head_dim):
    rows = seq // GRID_W
    row = jnp.repeat(jnp.arange(rows, dtype=F32), GRID_W)
    col = jnp.tile(jnp.arange(GRID_W, dtype=F32), rows)
    half = head_dim // 2
    pair = half // 2
    lane = jnp.arange(LANES)
    d = lane % head_dim
    inv = ROPE_BASE ** (-jnp.arange(pair, dtype=F32) / pair)
    freq = inv[d % pair]
    pos = jnp.where((d < half)[None, :], row[:, None], col[:, None])
    ang = pos * freq[None, :]
    cos = jnp.cos(ang)
    sin = jnp.sin(ang)
    first = ((d % half) < pair)[None, :]
    sin_a = jnp.where(first, -sin, 0.0)
    sin_b = jnp.where(first, 0.0, sin)
    return cos, sin_a, sin_b, pair


def _rope(acc, cos, sin_a, sin_b, pair):
    up = pltpu.roll(acc, LANES - pair, 1)
    dn = pltpu.roll(acc, pair, 1)
    return acc * cos + up * sin_a + dn * sin_b


def _inproj_kernel(*refs, specs, rope, ret_pair, diff_pair):
    if rope:
        (x_ref, g_ref, sc_ref, sh_ref, w_ref,
         rc_ref, ra_ref, rb_ref, dc_ref, da_ref, db_ref) = refs[:11]
        out_refs = refs[11:]
    else:
        x_ref, g_ref, sc_ref, sh_ref, w_ref = refs[:5]
        out_refs = refs[5:]
    sub = x_ref.shape[0] // INPROJ_SUBTILES
    for t in range(INPROJ_SUBTILES):
        rs = slice(t * sub, (t + 1) * sub)
        x = x_ref[rs, :]
        y = x * lax.rsqrt(jnp.mean(x * x, axis=-1, keepdims=True) + EPS)
        u = (y * g_ref[...]) * (1.0 + sc_ref[0]) + sh_ref[0]
        ub = u.astype(BF16)
        for (col0, width, kind), o_ref in zip(specs, out_refs):
            for c in range(0, width, INPROJ_COLS):
                cw = min(INPROJ_COLS, width - c)
                acc = jnp.dot(ub, w_ref[:, col0 + c:col0 + c + cw], preferred_element_type=F32)
                if kind in ("ret_q", "ret_k", "diff_q", "diff_k"):
                    for l in range(0, cw, LANES):
                        a = acc[:, l:l + LANES]
                        if kind.startswith("ret"):
                            r = _rope(a, rc_ref[rs, :], ra_ref[rs, :], rb_ref[rs, :], ret_pair)
                        else:
                            r = _rope(a, dc_ref[rs, :], da_ref[rs, :], db_ref[rs, :], diff_pair)
                        if kind == "ret_q":
                            r = r * (RET_DK ** -0.5)
                        elif kind == "diff_q":
                            r = r * (DIFF_D ** -0.5 * LOG2_E)
                        o_ref[rs, c + l:c + l + LANES] = r.astype(BF16)
                elif kind == "silu":
                    o_ref[rs, c:c + cw] = (acc * jax.nn.sigmoid(acc)).astype(BF16)
                elif kind == "sigmoid":
                    o_ref[rs, c:c + cw] = jax.nn.sigmoid(acc).astype(BF16)
                else:
                    o_ref[rs, c:c + cw] = acc.astype(BF16)


def _inproj(x2, gain, scale, shift, w_bf, specs, seq, tables):
    n, d = x2.shape
    tm = min(ROW_TILE, seq)
    assert seq % tm == 0 and n % seq == 0
    per_seq = seq // tm
    nb = scale.shape[0]
    if nb == 1:
        mod_map = lambda i: (0, 0, 0)
    else:
        mod_map = lambda i: (i // per_seq, 0, 0)
    rope = tables is not None
    in_specs = [pl.BlockSpec((tm, d), lambda i: (i, 0)),
                pl.BlockSpec((1, d), lambda i: (0, 0)),
                pl.BlockSpec((1, 1, d), mod_map),
                pl.BlockSpec((1, 1, d), mod_map),
                _resident(w_bf.shape)]
    args = [x2, gain.reshape(1, d), scale, shift, w_bf]
    ret_pair = diff_pair = 0
    if rope:
        (rc, ra, rb, ret_pair), (dc, da, db, diff_pair) = tables
        tab_spec = pl.BlockSpec((tm, LANES), lambda i: (i % per_seq, 0))
        in_specs += [tab_spec] * 6
        args += [rc, ra, rb, dc, da, db]
    out_specs = [pl.BlockSpec((tm, w), lambda i: (i, 0)) for (_, w, _) in specs]
    out_shape = [jax.ShapeDtypeStruct((n, w), BF16) for (_, w, _) in specs]
    kern = functools.partial(_inproj_kernel, specs=tuple(specs), rope=rope,
                             ret_pair=ret_pair, diff_pair=diff_pair)
    return pl.pallas_call(
        kern, grid=(n // tm,), in_specs=in_specs, out_specs=out_specs, out_shape=out_shape,
        compiler_params=_cparams("parallel"),
        name="inproj_rope" if rope else "inproj_ctx",
    )(*args)


def _ret_kernel(sdec_ref, q_ref, k_ref, v_ref, g_ref, kc_ref, vc_ref, m_ref, dec_ref, cdec_ref,
                gn_ref, o_ref, ybuf, kv, st, sf, sb, *, n_chunks):
    C = RET_CHUNK
    DK = RET_DK
    h = pl.program_id(1)
    sdec_f = sdec_ref[2 * h]
    sdec_b = sdec_ref[2 * h + 1]

    kc = kc_ref[...].astype(F32)
    vc = vc_ref[...]
    sf[...] = lax.dot_general((kc * cdec_ref[0, 0]).astype(BF16), vc, TN_DIMS,
                              preferred_element_type=F32)
    sb[...] = lax.dot_general((kc * cdec_ref[0, 1]).astype(BF16), vc, TN_DIMS,
                              preferred_element_type=F32)

    dmat = m_ref[0]
    qdec_f = dec_ref[0, 0]
    kdec_f = dec_ref[0, 1]
    qdec_b = dec_ref[0, 2]
    kdec_b = dec_ref[0, 3]
    gn = gn_ref[...]

    def intra(i, carry):
        r = pl.multiple_of(i * C, C)
        q = q_ref[pl.ds(r, C), :]
        k = k_ref[pl.ds(r, C), :]
        v = v_ref[pl.ds(r, C), :]
        kf = k.astype(F32)
        att = lax.dot_general(q, k, NT_DIMS, preferred_element_type=F32) * dmat
        ybuf[pl.ds(r, C), :] = jnp.dot(att.astype(BF16), v, preferred_element_type=F32)
        kk = jnp.concatenate([(kf * kdec_f).astype(BF16), (kf * kdec_b).astype(BF16)], axis=1)
        kv[i] = lax.dot_general(kk, v, TN_DIMS, preferred_element_type=F32)
        return carry

    lax.fori_loop(0, n_chunks, intra, 0, unroll=min(RET_UNROLL, n_chunks))

    def scan(t, carry):
        i = t
        st[i, :DK, :] = sf[...].astype(BF16)
        sf[...] = sdec_f * sf[...] + kv[i, :DK, :]
        j = n_chunks - 1 - t
        st[j, DK:, :] = sb[...].astype(BF16)
        sb[...] = sdec_b * sb[...] + kv[j, DK:, :]
        return carry

    lax.fori_loop(0, n_chunks, scan, 0)

    def cross(i, carry):
        r = pl.multiple_of(i * C, C)
        qf = q_ref[pl.ds(r, C), :].astype(F32)
        qq = jnp.concatenate([(qf * qdec_f).astype(BF16), (qf * qdec_b).astype(BF16)], axis=1)
        y = ybuf[pl.ds(r, C), :] + jnp.dot(qq, st[i], preferred_element_type=F32)
        mu = jnp.mean(y, axis=-1, keepdims=True)
        yc = y - mu
        var = jnp.mean(yc * yc, axis=-1, keepdims=True)
        yn = yc * lax.rsqrt(var + EPS) * gn
        o_ref[pl.ds(r, C), :] = (g_ref[pl.ds(r, C), :].astype(F32) * yn).astype(BF16)
        return carry

    lax.fori_loop(0, n_chunks, cross, 0, unroll=min(RET_UNROLL, n_chunks))


def _retention(ret_q, ret_k, ret_v, ret_g, ctx_rk, ctx_rv, dec_f, dec_b, gn_g, batch, seq, ctx_len):
    C = RET_CHUNK
    H = RET_HEADS
    lg_f = jnp.log1p(-jnp.exp2(dec_f.astype(F32)))
    lg_b = jnp.log1p(-jnp.exp2(dec_b.astype(F32)))
    idx = jnp.arange(C, dtype=F32)
    dist = idx[:, None] - idx[None, :]
    dmat = jnp.where(dist[None] >= 0,
                     jnp.exp(lg_f[:, None, None] * jnp.maximum(dist, 0.0)[None]),
                     jnp.exp(lg_b[:, None, None] * jnp.maximum(-dist, 0.0)[None]))
    qdec_f = jnp.exp(lg_f[:, None] * (idx + 1.0))
    kdec_f = jnp.exp(lg_f[:, None] * (C - 1.0 - idx))
    qdec_b = jnp.exp(lg_b[:, None] * (C - idx))
    kdec_b = jnp.exp(lg_b[:, None] * idx)
    dec = jnp.stack([qdec_f, kdec_f, qdec_b, kdec_b], axis=1)
    dec = jnp.broadcast_to(dec[..., None], (H, 4, C, RET_DK))
    cidx = jnp.arange(ctx_len, dtype=F32)
    cdec = jnp.stack([jnp.exp(lg_f[:, None] * (ctx_len - 1.0 - cidx)),
                      jnp.exp(lg_b[:, None] * cidx)], axis=1)
    cdec = jnp.broadcast_to(cdec[..., None], (H, 2, ctx_len, RET_DK))
    sdec = jnp.stack([jnp.exp(lg_f * C), jnp.exp(lg_b * C)], axis=1).reshape(2 * H)

    n_chunks = seq // C
    kern = functools.partial(_ret_kernel, n_chunks=n_chunks)
    return pl.pallas_call(
        kern,
        grid=(batch, H),
        in_specs=[pl.BlockSpec(memory_space=pltpu.SMEM),
                  pl.BlockSpec((seq, RET_DK), lambda b, h: (b, h)),
                  pl.BlockSpec((seq, RET_DK), lambda b, h: (b, h)),
                  pl.BlockSpec((seq, RET_DV), lambda b, h: (b, h)),
                  pl.BlockSpec((seq, RET_DV), lambda b, h: (b, h)),
                  pl.BlockSpec((ctx_len, RET_DK), lambda b, h: (b, h)),
                  pl.BlockSpec((ctx_len, RET_DV), lambda b, h: (b, h)),
                  pl.BlockSpec((1, C, C), lambda b, h: (h, 0, 0)),
                  pl.BlockSpec((1, 4, C, RET_DK), lambda b, h: (h, 0, 0, 0)),
                  pl.BlockSpec((1, 2, ctx_len, RET_DK), lambda b, h: (h, 0, 0, 0)),
                  pl.BlockSpec((1, RET_DV), lambda b, h: (0, h))],
        out_specs=pl.BlockSpec((seq, RET_DV), lambda b, h: (b, h)),
        out_shape=jax.ShapeDtypeStruct((batch * seq, RET_V_W), BF16),
        scratch_shapes=[pltpu.VMEM((seq, RET_DV), F32),
                        pltpu.VMEM((n_chunks, 2 * RET_DK, RET_DV), F32),
                        pltpu.VMEM((n_chunks, 2 * RET_DK, RET_DV), BF16),
                        pltpu.VMEM((RET_DK, RET_DV), F32),
                        pltpu.VMEM((RET_DK, RET_DV), F32)],
        compiler_params=_cparams("parallel", "arbitrary"),
        name="retention",
    )(sdec, ret_q, ret_k, ret_v, ret_g, ctx_rk, ctx_rv, dmat, dec, cdec,
      gn_g.reshape(1, RET_V_W))


def _diff_kernel(q_ref, kl_ref, kc_ref, vl_ref, vc_ref, lp_ref, gn_ref, o_ref, *, lam_init):
    kl = kl_ref[...]
    kc = kc_ref[...]
    tq, hw = q_ref.shape
    vl = jnp.concatenate([vl_ref[...], jnp.ones(vl_ref.shape, BF16)], axis=1)
    vc = jnp.concatenate([vc_ref[...], jnp.ones(vc_ref.shape, BF16)], axis=1)
    lp = lp_ref[...]
    lam = (jnp.exp(jnp.sum(lp[0:1] * lp[1:2], axis=-1, keepdims=True))
           - jnp.exp(jnp.sum(lp[2:3] * lp[3:4], axis=-1, keepdims=True)) + lam_init)

    def branch(qm):
        sl = lax.dot_general(qm, kl, NT_DIMS, preferred_element_type=F32)
        sc = lax.dot_general(qm, kc, NT_DIMS, preferred_element_type=F32)
        m = jnp.maximum(jnp.max(sl, axis=-1, keepdims=True), jnp.max(sc, axis=-1, keepdims=True))
        pl_ = jnp.exp2(sl - m)
        pc = jnp.exp2(sc - m)
        o = (jnp.dot(pl_.astype(BF16), vl, preferred_element_type=F32)
             + jnp.dot(pc.astype(BF16), vc, preferred_element_type=F32))
        return o[:, :hw] / o[:, hw:]

    sub = DIFF_SUB_ROWS
    for t in range(tq // sub):
        rows = slice(t * sub, (t + 1) * sub)
        q = q_ref[rows, :]
        lane = lax.broadcasted_iota(jnp.int32, q.shape, 1)
        zero = jnp.zeros_like(q)
        o1 = branch(jnp.where(lane < DIFF_D, q, zero))
        o2 = branch(jnp.where(lane >= DIFF_D, q, zero))
        o = o1 - lam * o2
        on = o * lax.rsqrt(jnp.mean(o * o, axis=-1, keepdims=True) + EPS) * (1.0 - lam_init)
        o_ref[rows, :] = (on * gn_ref[...]).astype(BF16)


def _diff_attention(dq, dk, dv, ctx_dk, ctx_dv, lam_params, gn_g, lam_init, batch, seq, ctx_len):
    tq = min(DIFF_Q_TILE, seq)
    nq = seq // tq
    hw = 2 * DIFF_D
    kern = functools.partial(_diff_kernel, lam_init=lam_init)
    return pl.pallas_call(
        kern,
        grid=(batch, DIFF_HEADS, nq),
        in_specs=[pl.BlockSpec((tq, hw), lambda b, h, i: (b * nq + i, h)),
                  pl.BlockSpec((seq, hw), lambda b, h, i: (b, h)),
                  pl.BlockSpec((ctx_len, hw), lambda b, h, i: (b, h)),
                  pl.BlockSpec((seq, hw), lambda b, h, i: (b, h)),
                  pl.BlockSpec((ctx_len, hw), lambda b, h, i: (b, h)),
                  pl.BlockSpec((4, DIFF_D), lambda b, h, i: (0, 0)),
                  pl.BlockSpec((1, hw), lambda b, h, i: (0, h))],
        out_specs=pl.BlockSpec((tq, hw), lambda b, h, i: (b * nq + i, h)),
        out_shape=jax.ShapeDtypeStruct((batch * seq, DIFF_W), BF16),
        compiler_params=_cparams("parallel", "parallel", "arbitrary"),
        name="diff_attention",
    )(dq, dk, ctx_dk, dv, ctx_dv, lam_params, gn_g.reshape(1, DIFF_W))


def _merge_kernel(x_ref, zr_ref, zd_ref, gr_ref, gd_ref, wr_ref, wd_ref, wo_ref,
                  pmg_ref, pfg_ref, ga_ref, shf_ref, scf_ref, x1_ref, f_ref):
    sub = x_ref.shape[0] // MERGE_SUBTILES
    for t in range(MERGE_SUBTILES):
        rows = slice(t * sub, (t + 1) * sub)
        p_ret = jnp.dot(zr_ref[rows, :], wr_ref[...], preferred_element_type=F32)
        p_diff = jnp.dot(zd_ref[rows, :], wd_ref[...], preferred_element_type=F32)
        m = gr_ref[rows, :].astype(F32) * p_ret + gd_ref[rows, :].astype(F32) * p_diff
        mix = jnp.dot(m.astype(BF16), wo_ref[...], preferred_element_type=F32)
        mixn = mix * lax.rsqrt(jnp.mean(mix * mix, axis=-1, keepdims=True) + EPS) * pmg_ref[...]
        x1 = x_ref[rows, :] + ga_ref[0] * mixn
        x1_ref[rows, :] = x1
        fn = x1 * lax.rsqrt(jnp.mean(x1 * x1, axis=-1, keepdims=True) + EPS) * pfg_ref[...]
        f_ref[rows, :] = (fn * (1.0 + scf_ref[0]) + shf_ref[0]).astype(BF16)


def _merge(x2, z_ret, z_diff, g_r, g_d, w_br_ret, w_br_diff, w_out, post_mix_g, pre_ffn_g,
           ga_a, sh_f, sc_f, seq):
    n, d = x2.shape
    tm = min(ROW_TILE, seq)
    per_seq = seq // tm
    row = lambda i: (i, 0)
    mod = lambda i: (i // per_seq, 0, 0)
    vec = pl.BlockSpec((1, d), lambda i: (0, 0))
    return pl.pallas_call(
        _merge_kernel,
        grid=(n // tm,),
        in_specs=[pl.BlockSpec((tm, d), row)] * 5
                 + [_resident((d, d))] * 3
                 + [vec, vec]
                 + [pl.BlockSpec((1, 1, d), mod)] * 3,
        out_specs=[pl.BlockSpec((tm, d), row), pl.BlockSpec((tm, d), row)],
        out_shape=[jax.ShapeDtypeStruct((n, d), F32), jax.ShapeDtypeStruct((n, d), BF16)],
        compiler_params=_cparams("parallel"),
        name="merge",
    )(x2, z_ret, z_diff, g_r, g_d, w_br_ret.astype(BF16), w_br_diff.astype(BF16),
      w_out.astype(BF16), post_mix_g.reshape(1, d), pre_ffn_g.reshape(1, d), ga_a, sh_f, sc_f)


def _staircase():
    return [(p, q) for p in range(PEER_TOPK) for q in range(PEER_TOPK)
            if (p + 1) * (q + 1) <= PEER_TOPK]


def _bf16_pair_word(x):
    hi = pltpu.bitcast(x.astype(BF16).astype(F32), jnp.uint32)
    return hi | (hi >> 16)


def _bf16_rows(word_row):
    w = jnp.broadcast_to(word_row, (BF16_ROWS // 2, word_row.shape[1]))
    return pltpu.bitcast(w, BF16)


def _count_leading(rows, test, like):
    assert len(rows) == 16
    count = jnp.zeros_like(like)
    decisions = []
    for width in (8, 4, 2, 1):
        cands = [rows[base + width - 1] for base in range(0, 16, 2 * width)]
        for bit in reversed(decisions):
            cands = [jnp.where(bit, hi, lo) for lo, hi in zip(cands[0::2], cands[1::2])]
        passed = test(cands[0])
        decisions.append(passed)
        count = count + jnp.where(passed, float(width), 0.0)
    return jnp.where(test(rows[15]), 16.0, count)


def _sort16_pairs():
    pairs = []

    def merge(lo, hi, r):
        step = r * 2
        if step < hi - lo:
            merge(lo, hi, step)
            merge(lo + r, hi, step)
            pairs.extend((i, i + r) for i in range(lo + r, hi - r, step))
        else:
            pairs.append((lo, lo + r))

    def sort(lo, hi):
        if hi - lo >= 1:
            mid = lo + (hi - lo) // 2
            sort(lo, mid)
            sort(mid + 1, hi)
            merge(lo, hi, 1)

    sort(0, 15)
    return pairs


def _cx(a, b):
    if b is None:
        return a, None
    if a is None:
        return b, None
    return jnp.maximum(a, b), jnp.minimum(a, b)


def _sort16(vals):
    cur = list(vals)
    for i, j in _sort16_pairs():
        cur[i], cur[j] = _cx(cur[i], cur[j])
    return cur


def _merge_top16(a, b):
    cur = [_cx(a[r], b[15 - r])[0] for r in range(16)]
    for stride in (8, 4, 2, 1):
        for i in range(16):
            if i & stride == 0:
                cur[i], cur[i + stride] = _cx(cur[i], cur[i + stride])
    return cur


def _top16_sorted(slabs):
    cur = _sort16(slabs)
    for shift in (4, 2, 1):
        cur = _merge_top16(cur, [pltpu.roll(x, shift, 0) for x in cur])
    return cur


def _route_kernel(fb_ref, wq_ref, sk_ref, r2_ref, n1_ref, e1_ref, e2_ref, s_scr, top):
    K = PEER_NKEYS
    G = 2 * PEER_HEADS
    H = PEER_HEADS
    tt = fb_ref.shape[0]
    qt = lax.dot_general(wq_ref[...], fb_ref[...], NT_DIMS, preferred_element_type=F32)
    for g in range(G):
        s_scr[g * K:(g + 1) * K, :] = jnp.dot(sk_ref[g], qt[g * K:(g + 1) * K, :].astype(BF16),
                                              preferred_element_type=F32)

    for g in range(G):
        hh, a = divmod(g, 2)
        for c in range(0, tt, LANES):
            lanes = slice(c, c + LANES)
            best = _top16_sorted([s_scr[g * K + r * SUBLANES:g * K + (r + 1) * SUBLANES, lanes]
                                  for r in range(PEER_TOPK)])
            for p in range(PEER_TOPK):
                row = (a * PEER_TOPK + p) * H + hh
                top[row:row + 1, lanes] = best[p][0:1, :]

    tops_a = [top[p * H:(p + 1) * H, :] for p in range(PEER_TOPK)]
    tops_b = [top[(PEER_TOPK + q) * H:(PEER_TOPK + q + 1) * H, :] for q in range(PEER_TOPK)]
    pairs = _staircase()
    cand = [tops_a[p] + tops_b[q] for (p, q) in pairs]
    assert [pq[0] for pq in pairs[:PEER_TOPK]] == [0] * PEER_TOPK
    padded = cand + [None] * (-len(cand) % PEER_TOPK)
    lists = [padded[:PEER_TOPK]] + [_sort16(padded[i:i + PEER_TOPK])
                                    for i in range(PEER_TOPK, len(padded), PEER_TOPK)]
    while len(lists) > 1:
        lists = [_merge_top16(lists[i], lists[i + 1]) if i + 1 < len(lists) else lists[i]
                 for i in range(0, len(lists), 2)]
    tau = lists[0][PEER_TOPK - 1]
    c00 = cand[0]
    z = functools.reduce(
        lambda a, b: a + b,
        [jnp.where(c >= tau, jnp.exp(c - c00), 0.0) for c in cand])
    zinv = 1.0 / z

    for hh in range(H):
        s1 = s_scr[(2 * hh) * K:(2 * hh + 1) * K, :]
        s2 = s_scr[(2 * hh + 1) * K:(2 * hh + 2) * K, :]
        tau_h = tau[hh:hh + 1, :]
        desc = [tops_b[q][hh:hh + 1, :] for q in range(PEER_TOPK)]
        cnt = _count_leading(desc, lambda b: s1 + b >= tau_h, s1)
        n1_ref[hh] = _bf16_pair_word(cnt)
        e1_ref[hh] = _bf16_pair_word(jnp.exp(s1 - tops_a[0][hh:hh + 1, :]) * zinv[hh:hh + 1, :])
        e2 = jnp.exp(s2 - tops_b[0][hh:hh + 1, :]).astype(BF16)
        e2_ref[hh] = pltpu.bitcast(e2, jnp.uint32)
        asc = [tops_b[PEER_TOPK - 1 - q][hh:hh + 1, :] for q in range(PEER_TOPK)]
        rank = float(PEER_TOPK) - _count_leading(asc, lambda b: s2 >= b, s2)
        r2_ref[hh] = pltpu.bitcast(rank.astype(BF16), jnp.uint32)


def _route(f, w_q, sub_keys, seq):
    n, d = f.shape
    tt = min(ROUTE_TOKENS, seq)
    H, K = PEER_HEADS, PEER_NKEYS
    wq_t = w_q.T.astype(BF16)
    sk = sub_keys.reshape(2 * H, K, PEER_HALF).astype(BF16)
    row_tab = jax.ShapeDtypeStruct((H, K, n), jnp.uint32)
    key_tab = jax.ShapeDtypeStruct((H, K // 2, n), jnp.uint32)
    row_spec = pl.BlockSpec((H, K, tt), lambda t: (0, 0, t))
    key_spec = pl.BlockSpec((H, K // 2, tt), lambda t: (0, 0, t))
    return pl.pallas_call(
        _route_kernel,
        grid=(n // tt,),
        in_specs=[pl.BlockSpec((tt, d), lambda t: (t, 0)),
                  _resident(wq_t.shape), _resident(sk.shape)],
        out_specs=[key_spec, row_spec, row_spec, key_spec],
        out_shape=[key_tab, row_tab, row_tab, key_tab],
        scratch_shapes=[pltpu.VMEM((2 * H * K, tt), F32),
                        pltpu.VMEM((2 * PEER_TOPK * H, tt), F32)],
        compiler_params=_cparams("parallel"),
        name="peer_route",
    )(f, wq_t, sk)


def _gelu(x):
    return 0.5 * x * (1.0 + lax.erf(x * (2.0 ** -0.5)))


def _expert_kernel(fb_ref, u_ref, vt_ref, r2_ref, e2_ref, n1_ref, e1_ref, x1_ref, g_ref, ga_ref,
                   o_ref, acc, h_scr, w_scr, *, rows_per_tile):
    e = pl.program_id(1)
    K = PEER_NKEYS

    @pl.when(e == 0)
    def _():
        acc[...] = jnp.zeros_like(acc)

    tt = fb_ref.shape[0]
    groups = K // BF16_ROWS
    lane_chunk = min(tt, EXPERT_LANE_CHUNK)
    zero = jnp.zeros((BF16_ROWS, lane_chunk), BF16)
    fb = fb_ref[...]
    per_chunk = rows_per_tile // EXPERT_ROW_CHUNKS
    for i in range(rows_per_tile):
        if i % per_chunk == 0:
            chunk = slice(i * K, (i + per_chunk) * K)
            h_scr[chunk, :] = lax.dot_general(u_ref[chunk, :], fb, NT_DIMS,
                                              preferred_element_type=F32).astype(BF16)
        for c in range(0, tt, lane_chunk):
            lanes = slice(c, c + lane_chunk)
            gate = [None] * groups
            for hh in range(PEER_HEADS):
                n1 = _bf16_rows(n1_ref[hh, i:i + 1, lanes])
                e1 = _bf16_rows(e1_ref[hh, i:i + 1, lanes])
                for r in range(groups):
                    words = slice(r * BF16_ROWS // 2, (r + 1) * BF16_ROWS // 2)
                    r2 = pltpu.bitcast(r2_ref[hh, words, lanes], BF16)
                    e2 = pltpu.bitcast(e2_ref[hh, words, lanes], BF16)
                    term = jnp.where(r2 < n1, e1 * e2, zero)
                    gate[r] = term if hh == 0 else gate[r] + term
            for r in range(groups):
                rows = slice(i * K + r * BF16_ROWS, i * K + (r + 1) * BF16_ROWS)
                w_scr[rows, lanes] = gate[r] * _gelu(h_scr[rows, lanes])
    acc[...] += jnp.dot(vt_ref[...], w_scr[...], preferred_element_type=F32)

    @pl.when(e == pl.num_programs(1) - 1)
    def _():
        y = acc[...]
        yn = y * lax.rsqrt(jnp.mean(y * y, axis=0, keepdims=True) + EPS)
        o_ref[...] = x1_ref[...] + ga_ref[0] * (yn.T * g_ref[...])


def _experts(fb, u_bf, vt_bf, r2, n1, e1, e2, x1, post_ffn_g, ga_f, seq):
    n, d = fb.shape
    tt = min(EXPERT_TOKENS, seq)
    per_seq = seq // tt
    rows_per_tile = EXPERT_TILE_ROWS
    et = rows_per_tile * PEER_NKEYS
    H, K = PEER_HEADS, PEER_NKEYS
    kern = functools.partial(_expert_kernel, rows_per_tile=rows_per_tile)
    token_tile = pl.BlockSpec((tt, d), lambda t, e: (t, 0), pipeline_mode=pl.Buffered(1))
    return pl.pallas_call(
        kern,
        grid=(n // tt, PEER_EXPERTS // et),
        in_specs=[token_tile,
                  pl.BlockSpec((et, d), lambda t, e: (e, 0)),
                  pl.BlockSpec((d, et), lambda t, e: (0, e)),
                  pl.BlockSpec((H, K // 2, tt), lambda t, e: (0, 0, t)),
                  pl.BlockSpec((H, K // 2, tt), lambda t, e: (0, 0, t)),
                  pl.BlockSpec((H, rows_per_tile, tt), lambda t, e: (0, e, t)),
                  pl.BlockSpec((H, rows_per_tile, tt), lambda t, e: (0, e, t)),
                  token_tile,
                  pl.BlockSpec((1, d), lambda t, e: (0, 0)),
                  pl.BlockSpec((1, 1, d), lambda t, e: (t // per_seq, 0, 0))],
        out_specs=pl.BlockSpec((tt, d), lambda t, e: (t, 0)),
        out_shape=jax.ShapeDtypeStruct((n, d), F32),
        scratch_shapes=[pltpu.VMEM((d, tt), F32), pltpu.VMEM((et, tt), BF16),
                        pltpu.VMEM((et, tt), BF16)],
        compiler_params=_cparams("parallel", "arbitrary", vmem_limit=EXPERT_VMEM_LIMIT),
        name="peer_experts",
    )(fb, u_bf, vt_bf, r2, e2, n1, e1, x1, post_ffn_g.reshape(1, d), ga_f)


def kernel(x, c, ctx, c_ctx, w_mod, b_mod, pre_mix_g, post_mix_g, pre_ffn_g, post_ffn_g, w_in,
           ret_decay_fwd, ret_decay_bwd, ret_gn_g, diff_lambda, diff_gn_g, w_br_ret, w_br_diff,
           w_out, peer_w_q, peer_sub_keys, peer_u, peer_v):
    batch, seq, d = x.shape
    ctx_len = ctx.shape[1]
    depth = w_mod.shape[0]
    assert depth == 1 and d == D_MODEL
    l = 0
    lam_init = 0.8 - 0.6 * math.exp(-0.3 * l)

    rows = ((batch + 1 + 7) // 8) * 8
    cc = jnp.zeros((rows, d), F32).at[:batch].set(c).at[batch].set(c_ctx)
    mod = _modulation(cc, w_mod[l], b_mod[l])
    sh_a, sc_a, ga_a, sh_f, sc_f, ga_f = [t[:batch, None, :] for t in jnp.split(mod, 6, axis=-1)]
    csh_a, csc_a = [t[batch:batch + 1, None, :] for t in jnp.split(mod, 6, axis=-1)[:2]]

    w_in_bf = w_in[l].astype(BF16)
    x2 = x.reshape(batch * seq, d)
    ctx2 = ctx.reshape(batch * ctx_len, d)

    lat_specs = [(COL_RQ, RET_QK_W, "ret_q"), (COL_RK, RET_QK_W, "ret_k"),
                 (COL_RV, RET_V_W, "plain"), (COL_RG, RET_V_W, "silu"),
                 (COL_DQ, DIFF_W, "diff_q"), (COL_DK, DIFF_W, "diff_k"),
                 (COL_DV, DIFF_W, "plain"), (COL_GR, D_MODEL, "sigmoid"),
                 (COL_GD, D_MODEL, "sigmoid")]
    tables = (_rope_tables(seq, RET_DK), _rope_tables(seq, DIFF_D))
    rq, rk, rv, rg, dq, dk, dv, g_r, g_d = _inproj(
        x2, pre_mix_g[l], sc_a, sh_a, w_in_bf, lat_specs, seq, tables)

    ctx_specs = [(COL_RK, RET_QK_W, "plain"), (COL_RV, RET_V_W, "plain"),
                 (COL_DK, DIFF_W, "plain"), (COL_DV, DIFF_W, "plain")]
    rk_c, rv_c, dk_c, dv_c = _inproj(ctx2, pre_mix_g[l], csc_a, csh_a, w_in_bf, ctx_specs,
                                     ctx_len, None)

    z_ret = _retention(rq, rk, rv, rg, rk_c, rv_c, ret_decay_fwd[l], ret_decay_bwd[l],
                       ret_gn_g[l], batch, seq, ctx_len)
    z_diff = _diff_attention(dq, dk, dv, dk_c, dv_c, diff_lambda[l], diff_gn_g[l], lam_init,
                             batch, seq, ctx_len)
    x1, fb = _merge(x2, z_ret, z_diff, g_r, g_d, w_br_ret[l], w_br_diff[l], w_out[l],
                   post_mix_g[l], pre_ffn_g[l], ga_a, sh_f, sc_f, seq)

    r2, n1, e1, e2 = _route(fb, peer_w_q[l], peer_sub_keys[l], seq)
    out = _experts(fb, peer_u[l].astype(BF16), peer_v[l].T.astype(BF16), r2, n1, e1, e2, x1,
                   post_ffn_g[l], ga_f, seq)
    return out.reshape(batch, seq, d)
```

```python
import functools
import math

import jax
import jax.numpy as jnp
from jax import lax
from jax.experimental import pallas as pl
from jax.experimental.pallas import tpu as pltpu

F32 = jnp.float32
BF16 = jnp.bfloat16

D_MODEL = 1024
GRID_W = 64
EPS = 1e-6
ROPE_BASE = 10000.0

RET_HEADS = 4
RET_DK = 128
RET_DV = 256
RET_CHUNK = 128
RET_QK_W = RET_HEADS * RET_DK
RET_V_W = RET_HEADS * RET_DV

DIFF_HEADS = 8
DIFF_D = 64
DIFF_W = DIFF_HEADS * 2 * DIFF_D

PEER_HEADS = 8
PEER_NKEYS = 128
PEER_EXPERTS = PEER_NKEYS * PEER_NKEYS
PEER_HALF = 128
PEER_TOPK = 16

COL_RQ = 0
COL_RK = COL_RQ + RET_QK_W
COL_RV = COL_RK + RET_QK_W
COL_RG = COL_RV + RET_V_W
COL_DQ = COL_RG + RET_V_W
COL_DK = COL_DQ + DIFF_W
COL_DV = COL_DK + DIFF_W
COL_GR = COL_DV + DIFF_W
COL_GD = COL_GR + D_MODEL
IN_COLS = COL_GD + D_MODEL

LANES = 128
SUBLANES = 8
BF16_ROWS = 2 * SUBLANES
V7X_VMEM_BYTES = 64 << 20

MOD_COLS = 768
ROW_TILE = 512
MERGE_ROW_TILE = 1024
INPROJ_SUBTILES = 2
INPROJ_COLS = 512
MERGE_SUBTILES = 4
RET_UNROLL = 16
DIFF_Q_TILE = 2048
DIFF_SUB_ROWS = 128
ROUTE_TOKENS = 256
EXPERT_TOKENS = 1024
EXPERT_LANE_CHUNK = 512
EXPERT_TILE_ROWS = 16
EXPERT_ROW_CHUNKS = 8
VMEM_LIMIT = V7X_VMEM_BYTES - (8 << 20)
EXPERT_VMEM_LIMIT = V7X_VMEM_BYTES - (6 << 20)

LOG2_E = math.log2(math.e)

NT_DIMS = (((1,), (1,)), ((), ()))
TN_DIMS = (((0,), (0,)), ((), ()))


def _cparams(*sem, vmem_limit=VMEM_LIMIT):
    return pltpu.CompilerParams(dimension_semantics=sem, vmem_limit_bytes=vmem_limit)


def _resident(shape):
    nd = len(shape)
    return pl.BlockSpec(shape, lambda *_: (0,) * nd, pipeline_mode=pl.Buffered(1))


def _mod_kernel(c_ref, w_ref, b_ref, o_ref):
    c = c_ref[...]
    s = c * jax.nn.sigmoid(c)
    o_ref[...] = jnp.dot(s, w_ref[...], preferred_element_type=F32,
                         precision=lax.Precision.HIGHEST) + b_ref[...]


def _modulation(cc, w, b):
    rows, d = cc.shape
    n = w.shape[1]
    tn = MOD_COLS
    return pl.pallas_call(
        _mod_kernel,
        grid=(n // tn,),
        in_specs=[pl.BlockSpec((rows, d), lambda j: (0, 0)),
                  pl.BlockSpec((d, tn), lambda j: (0, j)),
                  pl.BlockSpec((1, tn), lambda j: (0, j))],
        out_specs=pl.BlockSpec((rows, tn), lambda j: (0, j)),
        out_shape=jax.ShapeDtypeStruct((rows, n), F32),
        compiler_params=_cparams("arbitrary"),
        name="modulation",
    )(cc, w, b.reshape(1, n))


def _rope_tables(seq, head_dim):
    rows = seq // GRID_W
    row = jnp.repeat(jnp.arange(rows, dtype=F32), GRID_W)
    col = jnp.tile(jnp.arange(GRID_W, dtype=F32), rows)
    half = head_dim // 2
    pair = half // 2
    lane = jnp.arange(LANES)
    d = lane % head_dim
    inv = ROPE_BASE ** (-jnp.arange(pair, dtype=F32) / pair)
    freq = inv[d % pair]
    pos = jnp.where((d < half)[None, :], row[:, None], col[:, None])
    ang = pos * freq[None, :]
    cos = jnp.cos(ang)
    sin = jnp.sin(ang)
    first = ((d % half) < pair)[None, :]
    sin_a = jnp.where(first, -sin, 0.0)
    sin_b = jnp.where(first, 0.0, sin)
    return cos, sin_a, sin_b, pair


def _rope(acc, cos, sin_a, sin_b, pair):
    up = pltpu.roll(acc, LANES - pair, 1)
    dn = pltpu.roll(acc, pair, 1)
    return acc * cos + up * sin_a + dn * sin_b


def _inproj_kernel(*refs, specs, rope, ret_pair, diff_pair):
    if rope:
        (x_ref, g_ref, sc_ref, sh_ref, w_ref,
         rc_ref, ra_ref, rb_ref, dc_ref, da_ref, db_ref) = refs[:11]
        out_refs = refs[11:]
    else:
        x_ref, g_ref, sc_ref, sh_ref, w_ref = refs[:5]
        out_refs = refs[5:]
    sub = x_ref.shape[0] // INPROJ_SUBTILES
    for t in range(INPROJ_SUBTILES):
        rs = slice(t * sub, (t + 1) * sub)
        x = x_ref[rs, :]
        y = x * lax.rsqrt(jnp.mean(x * x, axis=-1, keepdims=True) + EPS)
        u = (y * g_ref[...]) * (1.0 + sc_ref[0]) + sh_ref[0]
        ub = u.astype(BF16)
        for (col0, width, kind), o_ref in zip(specs, out_refs):
            for c in range(0, width, INPROJ_COLS):
                cw = min(INPROJ_COLS, width - c)
                acc = jnp.dot(ub, w_ref[:, col0 + c:col0 + c + cw], preferred_element_type=F32)
                if kind in ("ret_q", "ret_k", "diff_q", "diff_k"):
                    for l in range(0, cw, LANES):
                        a = acc[:, l:l + LANES]
                        if kind.startswith("ret"):
                            r = _rope(a, rc_ref[rs, :], ra_ref[rs, :], rb_ref[rs, :], ret_pair)
                        else:
                            r = _rope(a, dc_ref[rs, :], da_ref[rs, :], db_ref[rs, :], diff_pair)
                        if kind == "ret_q":
                            r = r * (RET_DK ** -0.5)
                        elif kind == "diff_q":
                            r = r * (DIFF_D ** -0.5 * LOG2_E)
                        o_ref[rs, c + l:c + l + LANES] = r.astype(BF16)
                elif kind == "silu":
                    o_ref[rs, c:c + cw] = (acc * jax.nn.sigmoid(acc)).astype(BF16)
                elif kind == "sigmoid":
                    o_ref[rs, c:c + cw] = jax.nn.sigmoid(acc).astype(BF16)
                else:
                    o_ref[rs, c:c + cw] = acc.astype(BF16)


def _inproj(x2, gain, scale, shift, w_bf, specs, seq, tables):
    n, d = x2.shape
    tm = min(ROW_TILE, seq)
    assert seq % tm == 0 and n % seq == 0
    per_seq = seq // tm
    nb = scale.shape[0]
    if nb == 1:
        mod_map = lambda i: (0, 0, 0)
    else:
        mod_map = lambda i: (i // per_seq, 0, 0)
    rope = tables is not None
    in_specs = [pl.BlockSpec((tm, d), lambda i: (i, 0)),
                pl.BlockSpec((1, d), lambda i: (0, 0)),
                pl.BlockSpec((1, 1, d), mod_map),
                pl.BlockSpec((1, 1, d), mod_map),
                _resident(w_bf.shape)]
    args = [x2, gain.reshape(1, d), scale, shift, w_bf]
    ret_pair = diff_pair = 0
    if rope:
        (rc, ra, rb, ret_pair), (dc, da, db, diff_pair) = tables
        tab_spec = pl.BlockSpec((tm, LANES), lambda i: (i % per_seq, 0))
        in_specs += [tab_spec] * 6
        args += [rc, ra, rb, dc, da, db]
    out_specs = [pl.BlockSpec((tm, w), lambda i: (i, 0)) for (_, w, _) in specs]
    out_shape = [jax.ShapeDtypeStruct((n, w), BF16) for (_, w, _) in specs]
    kern = functools.partial(_inproj_kernel, specs=tuple(specs), rope=rope,
                             ret_pair=ret_pair, diff_pair=diff_pair)
    return pl.pallas_call(
        kern, grid=(n // tm,), in_specs=in_specs, out_specs=out_specs, out_shape=out_shape,
        compiler_params=_cparams("parallel"),
        name="inproj_rope" if rope else "inproj_ctx",
    )(*args)


def _ret_kernel(sdec_ref, q_ref, k_ref, v_ref, g_ref, kc_ref, vc_ref, m_ref, dec_ref, cdec_ref,
                gn_ref, o_ref, ybuf, kv, st, sf, sb, *, n_chunks):
    C = RET_CHUNK
    DK = RET_DK
    h = pl.program_id(1)
    sdec_f = sdec_ref[2 * h]
    sdec_b = sdec_ref[2 * h + 1]

    kc = kc_ref[...].astype(F32)
    vc = vc_ref[...]
    sf[...] = lax.dot_general((kc * cdec_ref[0, 0]).astype(BF16), vc, TN_DIMS,
                              preferred_element_type=F32)
    sb[...] = lax.dot_general((kc * cdec_ref[0, 1]).astype(BF16), vc, TN_DIMS,
                              preferred_element_type=F32)

    dmat = m_ref[0]
    qdec_f = dec_ref[0, 0]
    kdec_f = dec_ref[0, 1]
    qdec_b = dec_ref[0, 2]
    kdec_b = dec_ref[0, 3]
    gn = gn_ref[...]

    def intra(i, carry):
        r = pl.multiple_of(i * C, C)
        q = q_ref[pl.ds(r, C), :]
        k = k_ref[pl.ds(r, C), :]
        v = v_ref[pl.ds(r, C), :]
        kf = k.astype(F32)
        att = lax.dot_general(q, k, NT_DIMS, preferred_element_type=F32) * dmat
        ybuf[pl.ds(r, C), :] = jnp.dot(att.astype(BF16), v, preferred_element_type=F32)
        kk = jnp.concatenate([(kf * kdec_f).astype(BF16), (kf * kdec_b).astype(BF16)], axis=1)
        kv[i] = lax.dot_general(kk, v, TN_DIMS, preferred_element_type=F32)
        return carry

    lax.fori_loop(0, n_chunks, intra, 0, unroll=min(RET_UNROLL, n_chunks))

    def scan(t, carry):
        i = t
        st[i, :DK, :] = sf[...].astype(BF16)
        sf[...] = sdec_f * sf[...] + kv[i, :DK, :]
        j = n_chunks - 1 - t
        st[j, DK:, :] = sb[...].astype(BF16)
        sb[...] = sdec_b * sb[...] + kv[j, DK:, :]
        return carry

    lax.fori_loop(0, n_chunks, scan, 0, unroll=min(RET_UNROLL, n_chunks))

    def cross(i, carry):
        r = pl.multiple_of(i * C, C)
        qf = q_ref[pl.ds(r, C), :].astype(F32)
        qq = jnp.concatenate([(qf * qdec_f).astype(BF16), (qf * qdec_b).astype(BF16)], axis=1)
        y = ybuf[pl.ds(r, C), :] + jnp.dot(qq, st[i], preferred_element_type=F32)
        mu = jnp.mean(y, axis=-1, keepdims=True)
        yc = y - mu
        var = jnp.mean(yc * yc, axis=-1, keepdims=True)
        yn = yc * lax.rsqrt(var + EPS) * gn
        o_ref[pl.ds(r, C), :] = (g_ref[pl.ds(r, C), :].astype(F32) * yn).astype(BF16)
        return carry

    lax.fori_loop(0, n_chunks, cross, 0, unroll=min(RET_UNROLL, n_chunks))


def _retention(ret_q, ret_k, ret_v, ret_g, ctx_rk, ctx_rv, dec_f, dec_b, gn_g, batch, seq, ctx_len):
    C = RET_CHUNK
    H = RET_HEADS
    lg_f = jnp.log1p(-jnp.exp2(dec_f.astype(F32)))
    lg_b = jnp.log1p(-jnp.exp2(dec_b.astype(F32)))
    idx = jnp.arange(C, dtype=F32)
    dist = idx[:, None] - idx[None, :]
    dmat = jnp.where(dist[None] >= 0,
                     jnp.exp(lg_f[:, None, None] * jnp.maximum(dist, 0.0)[None]),
                     jnp.exp(lg_b[:, None, None] * jnp.maximum(-dist, 0.0)[None]))
    qdec_f = jnp.exp(lg_f[:, None] * (idx + 1.0))
    kdec_f = jnp.exp(lg_f[:, None] * (C - 1.0 - idx))
    qdec_b = jnp.exp(lg_b[:, None] * (C - idx))
    kdec_b = jnp.exp(lg_b[:, None] * idx)
    dec = jnp.stack([qdec_f, kdec_f, qdec_b, kdec_b], axis=1)
    dec = jnp.broadcast_to(dec[..., None], (H, 4, C, RET_DK))
    cidx = jnp.arange(ctx_len, dtype=F32)
    cdec = jnp.stack([jnp.exp(lg_f[:, None] * (ctx_len - 1.0 - cidx)),
                      jnp.exp(lg_b[:, None] * cidx)], axis=1)
    cdec = jnp.broadcast_to(cdec[..., None], (H, 2, ctx_len, RET_DK))
    sdec = jnp.stack([jnp.exp(lg_f * C), jnp.exp(lg_b * C)], axis=1).reshape(2 * H)

    n_chunks = seq // C
    kern = functools.partial(_ret_kernel, n_chunks=n_chunks)
    return pl.pallas_call(
        kern,
        grid=(batch, H),
        in_specs=[pl.BlockSpec(memory_space=pltpu.SMEM),
                  pl.BlockSpec((seq, RET_DK), lambda b, h: (b, h)),
                  pl.BlockSpec((seq, RET_DK), lambda b, h: (b, h)),
                  pl.BlockSpec((seq, RET_DV), lambda b, h: (b, h)),
                  pl.BlockSpec((seq, RET_DV), lambda b, h: (b, h)),
                  pl.BlockSpec((ctx_len, RET_DK), lambda b, h: (b, h)),
                  pl.BlockSpec((ctx_len, RET_DV), lambda b, h: (b, h)),
                  pl.BlockSpec((1, C, C), lambda b, h: (h, 0, 0)),
                  pl.BlockSpec((1, 4, C, RET_DK), lambda b, h: (h, 0, 0, 0)),
                  pl.BlockSpec((1, 2, ctx_len, RET_DK), lambda b, h: (h, 0, 0, 0)),
                  pl.BlockSpec((1, RET_DV), lambda b, h: (0, h))],
        out_specs=pl.BlockSpec((seq, RET_DV), lambda b, h: (b, h)),
        out_shape=jax.ShapeDtypeStruct((batch * seq, RET_V_W), BF16),
        scratch_shapes=[pltpu.VMEM((seq, RET_DV), F32),
                        pltpu.VMEM((n_chunks, 2 * RET_DK, RET_DV), F32),
                        pltpu.VMEM((n_chunks, 2 * RET_DK, RET_DV), BF16),
                        pltpu.VMEM((RET_DK, RET_DV), F32),
                        pltpu.VMEM((RET_DK, RET_DV), F32)],
        compiler_params=_cparams("parallel", "arbitrary"),
        name="retention",
    )(sdec, ret_q, ret_k, ret_v, ret_g, ctx_rk, ctx_rv, dmat, dec, cdec,
      gn_g.reshape(1, RET_V_W))


def _diff_kernel(q_ref, kl_ref, kc_ref, vl_ref, vc_ref, lp_ref, gn_ref, o_ref, *, lam_init):
    kl = kl_ref[...]
    kc = kc_ref[...]
    tq, hw = q_ref.shape
    vl = jnp.concatenate([vl_ref[...], jnp.ones(vl_ref.shape, BF16)], axis=1)
    vc = jnp.concatenate([vc_ref[...], jnp.ones(vc_ref.shape, BF16)], axis=1)
    lp = lp_ref[...]
    lam = (jnp.exp(jnp.sum(lp[0:1] * lp[1:2], axis=-1, keepdims=True))
           - jnp.exp(jnp.sum(lp[2:3] * lp[3:4], axis=-1, keepdims=True)) + lam_init)

    def branch(qm):
        sl = lax.dot_general(qm, kl, NT_DIMS, preferred_element_type=F32)
        sc = lax.dot_general(qm, kc, NT_DIMS, preferred_element_type=F32)
        m = jnp.maximum(jnp.max(sl, axis=-1, keepdims=True), jnp.max(sc, axis=-1, keepdims=True))
        pl_ = jnp.exp2(sl - m)
        pc = jnp.exp2(sc - m)
        o = (jnp.dot(pl_.astype(BF16), vl, preferred_element_type=F32)
             + jnp.dot(pc.astype(BF16), vc, preferred_element_type=F32))
        return o[:, :hw] / o[:, hw:]

    sub = DIFF_SUB_ROWS
    for t in range(tq // sub):
        rows = slice(t * sub, (t + 1) * sub)
        q = q_ref[rows, :]
        lane = lax.broadcasted_iota(jnp.int32, q.shape, 1)
        zero = jnp.zeros_like(q)
        o1 = branch(jnp.where(lane < DIFF_D, q, zero))
        o2 = branch(jnp.where(lane >= DIFF_D, q, zero))
        o = o1 - lam * o2
        on = o * lax.rsqrt(jnp.mean(o * o, axis=-1, keepdims=True) + EPS) * (1.0 - lam_init)
        o_ref[rows, :] = (on * gn_ref[...]).astype(BF16)


def _diff_attention(dq, dk, dv, ctx_dk, ctx_dv, lam_params, gn_g, lam_init, batch, seq, ctx_len):
    tq = min(DIFF_Q_TILE, seq)
    nq = seq // tq
    hw = 2 * DIFF_D
    kern = functools.partial(_diff_kernel, lam_init=lam_init)
    return pl.pallas_call(
        kern,
        grid=(batch, DIFF_HEADS, nq),
        in_specs=[pl.BlockSpec((tq, hw), lambda b, h, i: (b * nq + i, h)),
                  pl.BlockSpec((seq, hw), lambda b, h, i: (b, h)),
                  pl.BlockSpec((ctx_len, hw), lambda b, h, i: (b, h)),
                  pl.BlockSpec((seq, hw), lambda b, h, i: (b, h)),
                  pl.BlockSpec((ctx_len, hw), lambda b, h, i: (b, h)),
                  pl.BlockSpec((4, DIFF_D), lambda b, h, i: (0, 0)),
                  pl.BlockSpec((1, hw), lambda b, h, i: (0, h))],
        out_specs=pl.BlockSpec((tq, hw), lambda b, h, i: (b * nq + i, h)),
        out_shape=jax.ShapeDtypeStruct((batch * seq, DIFF_W), BF16),
        compiler_params=_cparams("parallel", "parallel", "arbitrary"),
        name="diff_attention",
    )(dq, dk, ctx_dk, dv, ctx_dv, lam_params, gn_g.reshape(1, DIFF_W))


def _merge_kernel(x_ref, zr_ref, zd_ref, gr_ref, gd_ref, wr_ref, wd_ref, wo_ref,
                  pmg_ref, pfg_ref, ga_ref, shf_ref, scf_ref, x1_ref, f_ref):
    sub = x_ref.shape[0] // MERGE_SUBTILES
    for t in range(MERGE_SUBTILES):
        rows = slice(t * sub, (t + 1) * sub)
        p_ret = jnp.dot(zr_ref[rows, :], wr_ref[...], preferred_element_type=F32)
        p_diff = jnp.dot(zd_ref[rows, :], wd_ref[...], preferred_element_type=F32)
        m = gr_ref[rows, :].astype(F32) * p_ret + gd_ref[rows, :].astype(F32) * p_diff
        mix = jnp.dot(m.astype(BF16), wo_ref[...], preferred_element_type=F32)
        mixn = mix * lax.rsqrt(jnp.mean(mix * mix, axis=-1, keepdims=True) + EPS) * pmg_ref[...]
        x1 = x_ref[rows, :] + ga_ref[0] * mixn
        x1_ref[rows, :] = x1
        fn = x1 * lax.rsqrt(jnp.mean(x1 * x1, axis=-1, keepdims=True) + EPS) * pfg_ref[...]
        f_ref[rows, :] = (fn * (1.0 + scf_ref[0]) + shf_ref[0]).astype(BF16)


def _merge(x2, z_ret, z_diff, g_r, g_d, w_br_ret, w_br_diff, w_out, post_mix_g, pre_ffn_g,
           ga_a, sh_f, sc_f, seq):
    n, d = x2.shape
    tm = min(MERGE_ROW_TILE, seq)
    per_seq = seq // tm
    row = lambda i: (i, 0)
    mod = lambda i: (i // per_seq, 0, 0)
    vec = pl.BlockSpec((1, d), lambda i: (0, 0))
    return pl.pallas_call(
        _merge_kernel,
        grid=(n // tm,),
        in_specs=[pl.BlockSpec((tm, d), row)] * 5
                 + [_resident((d, d))] * 3
                 + [vec, vec]
                 + [pl.BlockSpec((1, 1, d), mod)] * 3,
        out_specs=[pl.BlockSpec((tm, d), row), pl.BlockSpec((tm, d), row)],
        out_shape=[jax.ShapeDtypeStruct((n, d), F32), jax.ShapeDtypeStruct((n, d), BF16)],
        compiler_params=_cparams("parallel"),
        name="merge",
    )(x2, z_ret, z_diff, g_r, g_d, w_br_ret.astype(BF16), w_br_diff.astype(BF16),
      w_out.astype(BF16), post_mix_g.reshape(1, d), pre_ffn_g.reshape(1, d), ga_a, sh_f, sc_f)


def _staircase():
    return [(p, q) for p in range(PEER_TOPK) for q in range(PEER_TOPK)
            if (p + 1) * (q + 1) <= PEER_TOPK]


def _bf16_pair_word(x):
    hi = pltpu.bitcast(x.astype(BF16).astype(F32), jnp.uint32)
    return hi | (hi >> 16)


def _bf16_rows(word_row):
    w = jnp.broadcast_to(word_row, (BF16_ROWS // 2, word_row.shape[1]))
    return pltpu.bitcast(w, BF16)


def _count_leading(rows, test, like):
    assert len(rows) == 16
    count = jnp.zeros_like(like)
    decisions = []
    for width in (8, 4, 2, 1):
        cands = [rows[base + width - 1] for base in range(0, 16, 2 * width)]
        for bit in reversed(decisions):
            cands = [jnp.where(bit, hi, lo) for lo, hi in zip(cands[0::2], cands[1::2])]
        passed = test(cands[0])
        decisions.append(passed)
        count = count + jnp.where(passed, float(width), 0.0)
    return jnp.where(test(rows[15]), 16.0, count)


def _sort16_pairs():
    pairs = []

    def merge(lo, hi, r):
        step = r * 2
        if step < hi - lo:
            merge(lo, hi, step)
            merge(lo + r, hi, step)
            pairs.extend((i, i + r) for i in range(lo + r, hi - r, step))
        else:
            pairs.append((lo, lo + r))

    def sort(lo, hi):
        if hi - lo >= 1:
            mid = lo + (hi - lo) // 2
            sort(lo, mid)
            sort(mid + 1, hi)
            merge(lo, hi, 1)

    sort(0, 15)
    return pairs


def _cx(a, b):
    if b is None:
        return a, None
    if a is None:
        return b, None
    return jnp.maximum(a, b), jnp.minimum(a, b)


def _sort16(vals):
    cur = list(vals)
    for i, j in _sort16_pairs():
        cur[i], cur[j] = _cx(cur[i], cur[j])
    return cur


def _merge_top16(a, b):
    cur = [_cx(a[r], b[15 - r])[0] for r in range(16)]
    for stride in (8, 4, 2, 1):
        for i in range(16):
            if i & stride == 0:
                cur[i], cur[i + stride] = _cx(cur[i], cur[i + stride])
    return cur


def _top16_sorted(slabs):
    cur = _sort16(slabs)
    for shift in (4, 2, 1):
        cur = _merge_top16(cur, [pltpu.roll(x, shift, 0) for x in cur])
    return cur


def _route_kernel(fb_ref, wq_ref, sk_ref, r2_ref, n1_ref, e1_ref, e2_ref, s_scr, top):
    K = PEER_NKEYS
    G = 2 * PEER_HEADS
    H = PEER_HEADS
    tt = fb_ref.shape[0]
    qt = lax.dot_general(wq_ref[...], fb_ref[...], NT_DIMS, preferred_element_type=F32)
    for g in range(G):
        s_scr[g * K:(g + 1) * K, :] = jnp.dot(sk_ref[g], qt[g * K:(g + 1) * K, :].astype(BF16),
                                              preferred_element_type=F32)

    for g in range(G):
        hh, a = divmod(g, 2)
        for c in range(0, tt, LANES):
            lanes = slice(c, c + LANES)
            best = _top16_sorted([s_scr[g * K + r * SUBLANES:g * K + (r + 1) * SUBLANES, lanes]
                                  for r in range(PEER_TOPK)])
            for p in range(PEER_TOPK):
                row = (a * PEER_TOPK + p) * H + hh
                top[row:row + 1, lanes] = best[p][0:1, :]

    tops_a = [top[p * H:(p + 1) * H, :] for p in range(PEER_TOPK)]
    tops_b = [top[(PEER_TOPK + q) * H:(PEER_TOPK + q + 1) * H, :] for q in range(PEER_TOPK)]
    pairs = _staircase()
    cand = [tops_a[p] + tops_b[q] for (p, q) in pairs]
    assert [pq[0] for pq in pairs[:PEER_TOPK]] == [0] * PEER_TOPK
    padded = cand + [None] * (-len(cand) % PEER_TOPK)
    lists = [padded[:PEER_TOPK]] + [_sort16(padded[i:i + PEER_TOPK])
                                    for i in range(PEER_TOPK, len(padded), PEER_TOPK)]
    while len(lists) > 1:
        lists = [_merge_top16(lists[i], lists[i + 1]) if i + 1 < len(lists) else lists[i]
                 for i in range(0, len(lists), 2)]
    tau = lists[0][PEER_TOPK - 1]
    c00 = cand[0]
    z = functools.reduce(
        lambda a, b: a + b,
        [jnp.where(c >= tau, jnp.exp(c - c00), 0.0) for c in cand])
    zinv = 1.0 / z

    for hh in range(H):
        s1 = s_scr[(2 * hh) * K:(2 * hh + 1) * K, :]
        s2 = s_scr[(2 * hh + 1) * K:(2 * hh + 2) * K, :]
        tau_h = tau[hh:hh + 1, :]
        desc = [tops_b[q][hh:hh + 1, :] for q in range(PEER_TOPK)]
        cnt = _count_leading(desc, lambda b: s1 + b >= tau_h, s1)
        n1_ref[hh] = _bf16_pair_word(cnt)
        e1_ref[hh] = _bf16_pair_word(jnp.exp(s1 - tops_a[0][hh:hh + 1, :]) * zinv[hh:hh + 1, :])
        e2 = jnp.exp(s2 - tops_b[0][hh:hh + 1, :]).astype(BF16)
        e2_ref[hh] = pltpu.bitcast(e2, jnp.uint32)
        asc = [tops_b[PEER_TOPK - 1 - q][hh:hh + 1, :] for q in range(PEER_TOPK)]
        rank = float(PEER_TOPK) - _count_leading(asc, lambda b: s2 >= b, s2)
        r2_ref[hh] = pltpu.bitcast(rank.astype(BF16), jnp.uint32)


def _route(f, w_q, sub_keys, seq):
    n, d = f.shape
    tt = min(ROUTE_TOKENS, seq)
    H, K = PEER_HEADS, PEER_NKEYS
    wq_t = w_q.T.astype(BF16)
    sk = sub_keys.reshape(2 * H, K, PEER_HALF).astype(BF16)
    row_tab = jax.ShapeDtypeStruct((H, K, n), jnp.uint32)
    key_tab = jax.ShapeDtypeStruct((H, K // 2, n), jnp.uint32)
    row_spec = pl.BlockSpec((H, K, tt), lambda t: (0, 0, t))
    key_spec = pl.BlockSpec((H, K // 2, tt), lambda t: (0, 0, t))
    return pl.pallas_call(
        _route_kernel,
        grid=(n // tt,),
        in_specs=[pl.BlockSpec((tt, d), lambda t: (t, 0)),
                  _resident(wq_t.shape), _resident(sk.shape)],
        out_specs=[key_spec, row_spec, row_spec, key_spec],
        out_shape=[key_tab, row_tab, row_tab, key_tab],
        scratch_shapes=[pltpu.VMEM((2 * H * K, tt), F32),
                        pltpu.VMEM((2 * PEER_TOPK * H, tt), F32)],
        compiler_params=_cparams("parallel"),
        name="peer_route",
    )(f, wq_t, sk)


def _gelu(x):
    return 0.5 * x * (1.0 + lax.erf(x * (2.0 ** -0.5)))


def _expert_kernel(fb_ref, u_ref, vt_ref, r2_ref, e2_ref, n1_ref, e1_ref, x1_ref, g_ref, ga_ref,
                   o_ref, acc, h_scr, w_scr, *, rows_per_tile):
    e = pl.program_id(1)
    K = PEER_NKEYS

    @pl.when(e == 0)
    def _():
        acc[...] = jnp.zeros_like(acc)

    tt = fb_ref.shape[0]
    groups = K // BF16_ROWS
    lane_chunk = min(tt, EXPERT_LANE_CHUNK)
    zero = jnp.zeros((BF16_ROWS, lane_chunk), BF16)
    fb = fb_ref[...]
    per_chunk = rows_per_tile // EXPERT_ROW_CHUNKS
    for i in range(rows_per_tile):
        if i % per_chunk == 0:
            chunk = slice(i * K, (i + per_chunk) * K)
            h_scr[chunk, :] = lax.dot_general(u_ref[chunk, :], fb, NT_DIMS,
                                              preferred_element_type=F32).astype(BF16)
        for c in range(0, tt, lane_chunk):
            lanes = slice(c, c + lane_chunk)
            gate = [None] * groups
            for hh in range(PEER_HEADS):
                n1 = _bf16_rows(n1_ref[hh, i:i + 1, lanes])
                e1 = _bf16_rows(e1_ref[hh, i:i + 1, lanes])
                for r in range(groups):
                    words = slice(r * BF16_ROWS // 2, (r + 1) * BF16_ROWS // 2)
                    r2 = pltpu.bitcast(r2_ref[hh, words, lanes], BF16)
                    e2 = pltpu.bitcast(e2_ref[hh, words, lanes], BF16)
                    term = jnp.where(r2 < n1, e1 * e2, zero)
                    gate[r] = term if hh == 0 else gate[r] + term
            for r in range(groups):
                rows = slice(i * K + r * BF16_ROWS, i * K + (r + 1) * BF16_ROWS)
                w_scr[rows, lanes] = gate[r] * _gelu(h_scr[rows, lanes])
    acc[...] += jnp.dot(vt_ref[...], w_scr[...], preferred_element_type=F32)

    @pl.when(e == pl.num_programs(1) - 1)
    def _():
        y = acc[...]
        yn = y * lax.rsqrt(jnp.mean(y * y, axis=0, keepdims=True) + EPS)
        o_ref[...] = x1_ref[...] + ga_ref[0] * (yn.T * g_ref[...])


def _experts(fb, u_bf, vt_bf, r2, n1, e1, e2, x1, post_ffn_g, ga_f, seq):
    n, d = fb.shape
    tt = min(EXPERT_TOKENS, seq)
    per_seq = seq // tt
    rows_per_tile = EXPERT_TILE_ROWS
    et = rows_per_tile * PEER_NKEYS
    H, K = PEER_HEADS, PEER_NKEYS
    kern = functools.partial(_expert_kernel, rows_per_tile=rows_per_tile)
    token_tile = pl.BlockSpec((tt, d), lambda t, e: (t, 0), pipeline_mode=pl.Buffered(1))
    return pl.pallas_call(
        kern,
        grid=(n // tt, PEER_EXPERTS // et),
        in_specs=[token_tile,
                  pl.BlockSpec((et, d), lambda t, e: (e, 0)),
                  pl.BlockSpec((d, et), lambda t, e: (0, e)),
                  pl.BlockSpec((H, K // 2, tt), lambda t, e: (0, 0, t)),
                  pl.BlockSpec((H, K // 2, tt), lambda t, e: (0, 0, t)),
                  pl.BlockSpec((H, rows_per_tile, tt), lambda t, e: (0, e, t)),
                  pl.BlockSpec((H, rows_per_tile, tt), lambda t, e: (0, e, t)),
                  token_tile,
                  pl.BlockSpec((1, d), lambda t, e: (0, 0)),
                  pl.BlockSpec((1, 1, d), lambda t, e: (t // per_seq, 0, 0))],
        out_specs=pl.BlockSpec((tt, d), lambda t, e: (t, 0)),
        out_shape=jax.ShapeDtypeStruct((n, d), F32),
        scratch_shapes=[pltpu.VMEM((d, tt), F32), pltpu.VMEM((et, tt), BF16),
                        pltpu.VMEM((et, tt), BF16)],
        compiler_params=_cparams("parallel", "arbitrary", vmem_limit=EXPERT_VMEM_LIMIT),
        name="peer_experts",
    )(fb, u_bf, vt_bf, r2, e2, n1, e1, x1, post_ffn_g.reshape(1, d), ga_f)


def kernel(x, c, ctx, c_ctx, w_mod, b_mod, pre_mix_g, post_mix_g, pre_ffn_g, post_ffn_g, w_in,
           ret_decay_fwd, ret_decay_bwd, ret_gn_g, diff_lambda, diff_gn_g, w_br_ret, w_br_diff,
           w_out, peer_w_q, peer_sub_keys, peer_u, peer_v):
    batch, seq, d = x.shape
    ctx_len = ctx.shape[1]
    depth = w_mod.shape[0]
    assert depth == 1 and d == D_MODEL
    l = 0
    lam_init = 0.8 - 0.6 * math.exp(-0.3 * l)

    rows = ((batch + 1 + 7) // 8) * 8
    cc = jnp.zeros((rows, d), F32).at[:batch].set(c).at[batch].set(c_ctx)
    mod = _modulation(cc, w_mod[l], b_mod[l])
    sh_a, sc_a, ga_a, sh_f, sc_f, ga_f = [t[:batch, None, :] for t in jnp.split(mod, 6, axis=-1)]
    csh_a, csc_a = [t[batch:batch + 1, None, :] for t in jnp.split(mod, 6, axis=-1)[:2]]

    w_in_bf = w_in[l].astype(BF16)
    x2 = x.reshape(batch * seq, d)
    ctx2 = ctx.reshape(batch * ctx_len, d)

    lat_specs = [(COL_RQ, RET_QK_W, "ret_q"), (COL_RK, RET_QK_W, "ret_k"),
                 (COL_RV, RET_V_W, "plain"), (COL_RG, RET_V_W, "silu"),
                 (COL_DQ, DIFF_W, "diff_q"), (COL_DK, DIFF_W, "diff_k"),
                 (COL_DV, DIFF_W, "plain"), (COL_GR, D_MODEL, "sigmoid"),
                 (COL_GD, D_MODEL, "sigmoid")]
    tables = (_rope_tables(seq, RET_DK), _rope_tables(seq, DIFF_D))
    rq, rk, rv, rg, dq, dk, dv, g_r, g_d = _inproj(
        x2, pre_mix_g[l], sc_a, sh_a, w_in_bf, lat_specs, seq, tables)

    ctx_specs = [(COL_RK, RET_QK_W, "plain"), (COL_RV, RET_V_W, "plain"),
                 (COL_DK, DIFF_W, "plain"), (COL_DV, DIFF_W, "plain")]
    rk_c, rv_c, dk_c, dv_c = _inproj(ctx2, pre_mix_g[l], csc_a, csh_a, w_in_bf, ctx_specs,
                                     ctx_len, None)

    z_ret = _retention(rq, rk, rv, rg, rk_c, rv_c, ret_decay_fwd[l], ret_decay_bwd[l],
                       ret_gn_g[l], batch, seq, ctx_len)
    z_diff = _diff_attention(dq, dk, dv, dk_c, dv_c, diff_lambda[l], diff_gn_g[l], lam_init,
                             batch, seq, ctx_len)
    x1, fb = _merge(x2, z_ret, z_diff, g_r, g_d, w_br_ret[l], w_br_diff[l], w_out[l],
                   post_mix_g[l], pre_ffn_g[l], ga_a, sh_f, sc_f, seq)

    r2, n1, e1, e2 = _route(fb, peer_w_q[l], peer_sub_keys[l], seq)
    out = _experts(fb, peer_u[l].astype(BF16), peer_v[l].T.astype(BF16), r2, n1, e1, e2, x1,
                   post_ffn_g[l], ga_f, seq)
    return out.reshape(batch, seq, d)
```
